```python
import jax, jax.numpy as jnp
from jax import lax
import numpy as np

D_MODEL = 1024
BATCH = 16
SEQ = 2048
DEPTH = 2
DEC_BATCH = 32
DEC_SEQ = 32
PAST_LEN = 2048

CHUNK = 64
N_EVEN = (DEPTH + 1) // 2
N_ODD = DEPTH // 2
EPS = 1e-6
FFN_SCALE = 0.5
D_FF = 2816
H_A = 8
DK_A = D_MODEL // H_A
DV_A = D_MODEL // H_A
HGRN_BLOCK = 16
H_B = 8
DH_B = D_MODEL // H_B
Q_BLOCK = 128
GMLP_LEN = 128
D_C = 2 * D_MODEL
G_C = 8
DG_C = D_C // G_C
EVEN_SIZES = (H_A * DK_A, H_A * DK_A, H_A * DV_A, H_A * DV_A, H_B * DH_B, H_B * DH_B, H_B * DH_B)
D_IN_EVEN = 2 * H_A * DK_A + 2 * H_A * DV_A + 3 * H_B * DH_B
D_MIX_EVEN = H_A * DV_A + H_B * DH_B

kernel_name = 'hybrid_hgrn2_stickbreak_gmlp_stream_step'


def _split_points(sizes):
    pts, acc = [], 0
    for s in sizes[:-1]:
        acc += s
        pts.append(acc)
    return pts


def _rms(x, g):
    xf = x.astype(jnp.float32)
    y = xf * lax.rsqrt(jnp.mean(jnp.square(xf), axis=-1, keepdims=True) + EPS)
    return (y * g.astype(jnp.float32)).astype(x.dtype)


def _layernorm(x, g, b):
    xf = x.astype(jnp.float32)
    xc = xf - jnp.mean(xf, axis=-1, keepdims=True)
    y = xc * lax.rsqrt(jnp.mean(jnp.square(xc), axis=-1, keepdims=True) + EPS)
    return (y * g.astype(jnp.float32) + b.astype(jnp.float32)).astype(x.dtype)


def _swiglu(x, w_gate, w_up, w_down):
    return (jax.nn.silu(x @ w_gate) * (x @ w_up)) @ w_down


def _hgrn_lower_bounds(lb_param):
    p = jax.nn.softmax(lb_param.astype(jnp.float32), axis=0)
    return jnp.cumsum(p, axis=0)[:-1]


def _hgrn2(q, k, v, log_f, s0):
    B, T, H, _ = q.shape
    L = HGRN_BLOCK
    n_blk = -(-T // L)
    pad = n_blk * L - T

    def prep(a):
        a = jnp.pad(a, ((0, 0), (0, pad), (0, 0), (0, 0)))
        return jnp.moveaxis(a.reshape(B, n_blk, L, H, a.shape[-1]), 1, 0)

    causal = jnp.tril(jnp.ones((L, L), dtype=bool))

    def step(S, blk):
        qb, kb, vb, lfb = blk
        c = jnp.cumsum(lfb, axis=1)
        diff = c[:, :, None] - c[:, None, :]
        decay = jnp.exp(jnp.where(causal[None, :, :, None, None], diff, -jnp.inf))
        scores = jnp.einsum('bthc,btshc,bshc->bhts', qb, decay, kb)
        o = jnp.einsum('bhts,bshd->bthd', scores, vb) + jnp.einsum('bthc,bhcd->bthd', qb * jnp.exp(c), S)
        c_last = c[:, -1]
        k_dec = kb * jnp.exp(c_last[:, None] - c)
        S_new = jnp.exp(c_last)[..., None] * S + jnp.einsum('bshc,bshd->bhcd', k_dec, vb)
        return S_new, o

    S_fin, o = lax.scan(step, s0, (prep(q), prep(k), prep(v), prep(log_f)))
    o = jnp.moveaxis(o, 0, 1).reshape(B, n_blk * L, H, v.shape[-1])[:, :T]
    return o, S_fin


def _stick_breaking(q, q_pos, k, v, k_pos):
    z = jnp.einsum('bqhd,bkhd->bhqk', q, k, preferred_element_type=jnp.float32) * (DH_B ** -0.5)
    mask = (k_pos[None, :] < q_pos[:, None])[None, None]
    log_stay = jnp.where(mask, jax.nn.log_sigmoid(-z), 0.0)
    log_pass = lax.cumsum(log_stay, axis=3, reverse=True) - log_stay
    w = jnp.where(mask, jnp.exp(jax.nn.log_sigmoid(z) + log_pass), 0.0)
    return jnp.einsum('bhqk,bkhd->bqhd', w.astype(v.dtype), v, preferred_element_type=jnp.float32)


def _mixer_ab(h, w_in, lb, g_out, g_q, g_k, w_out, s0, k_past, v_past):
    B, T, _ = h.shape
    f32 = jnp.float32
    qa, fa, va, ga, qb, kb, vb = jnp.split(h @ w_in, _split_points(EVEN_SIZES), axis=-1)
    heads_a = lambda a: a.reshape(B, T, H_A, -1)
    f = lb + (1.0 - lb) * jax.nn.sigmoid(fa.astype(f32))
    if s0 is None:
        s0 = jnp.zeros((B, H_A, DK_A, DV_A), f32)
    o_a, s_new = _hgrn2(heads_a(jax.nn.silu(qa).astype(f32)), heads_a(1.0 - f),
                        heads_a(va.astype(f32)), heads_a(jnp.log(f)), s0.astype(f32))
    o_a = _rms(o_a, g_out) * jax.nn.sigmoid(heads_a(ga.astype(f32)))
    qb = _rms(qb.reshape(B, T, H_B, DH_B), g_q)
    kb = _rms(kb.reshape(B, T, H_B, DH_B), g_k)
    vb = vb.reshape(B, T, H_B, DH_B)
    if k_past is None:
        nq = T // Q_BLOCK
        q_blocks = jnp.moveaxis(qb.reshape(B, nq, Q_BLOCK, H_B, DH_B), 1, 0)
        q_pos = jnp.arange(T).reshape(nq, Q_BLOCK)
        k_pos = jnp.arange(T)
        o_b = lax.map(lambda a: _stick_breaking(a[0], a[1], kb, vb, k_pos), (q_blocks, q_pos))
        o_b = jnp.moveaxis(o_b, 0, 1).reshape(B, T, H_B, DH_B)
    else:
        P = k_past.shape[1]
        k_all = jnp.concatenate([k_past.astype(kb.dtype), kb], axis=1)
        v_all = jnp.concatenate([v_past.astype(vb.dtype), vb], axis=1)
        o_b = _stick_breaking(qb, P + jnp.arange(T), k_all, v_all, jnp.arange(P + T))
    mix = jnp.concatenate([o_a.reshape(B, T, -1).astype(h.dtype), o_b.reshape(B, T, -1).astype(h.dtype)], axis=-1)
    return mix @ w_out, s_new.astype(h.dtype), kb, vb


def _mixer_c(h, w_in, ln_g, ln_b, w_s, b_s, w_out):
    B, T, _ = h.shape
    u, v = jnp.split(jax.nn.gelu(h @ w_in), 2, axis=-1)
    v = _layernorm(v, ln_g, ln_b)
    L = min(T, GMLP_LEN)
    n = T // L
    pos = jnp.arange(L)
    mask = (pos[None, :] // CHUNK) <= (pos[:, None] // CHUNK)
    ws = jnp.where(mask[None], w_s[:, :L, :L], 0.0).astype(v.dtype)
    vg = v.reshape(B, n, L, G_C, DG_C)
    mixed = jnp.einsum('gij,bnjgc->bnigc', ws, vg) + b_s[:, :L].T[None, None, :, :, None].astype(v.dtype)
    return (u * mixed.reshape(B, T, D_C)) @ w_out, v


def _trunk(x, p, hgrn_state, sb_k, sb_v):
    lbs = _hgrn_lower_bounds(p['ab_lb'])
    new_k, new_v, new_s, new_cv = [], [], [], []
    for l in range(DEPTH):
        x = x + FFN_SCALE * _swiglu(_rms(x, p['ffn1_norm'][l]), p['ffn1_w_gate'][l], p['ffn1_w_up'][l], p['ffn1_w_down'][l])
        h = _rms(x, p['mix_norm'][l])
        if l % 2 == 0:
            e = l // 2
            y, s, k, v = _mixer_ab(h, p['ab_w_in'][e], lbs[e], p['ab_g_out'][e], p['ab_g_q'][e], p['ab_g_k'][e],
                                   p['ab_w_out'][e],
                                   None if hgrn_state is None else hgrn_state[e],
                                   None if sb_k is None else sb_k[e],
                                   None if sb_v is None else sb_v[e])
            new_s.append(s)
            new_k.append(k)
            new_v.append(v)
        else:
            o = l // 2
            y, cv = _mixer_c(h, p['c_w_in'][o], p['c_ln_g'][o], p['c_ln_b'][o], p['c_w_s'][o], p['c_b_s'][o], p['c_w_out'][o])
            new_cv.append(cv)
        x = x + y
        x = x + FFN_SCALE * _swiglu(_rms(x, p['ffn2_norm'][l]), p['ffn2_w_gate'][l], p['ffn2_w_up'][l], p['ffn2_w_down'][l])
    return x, new_k, new_v, new_s, new_cv


def setup_inputs(seed: int = 0) -> dict:
    key = jax.random.key(seed)
    ks = iter(jax.random.split(key, 40))
    nrm = lambda shape, scale: jax.random.normal(next(ks), shape, jnp.float32) * scale
    gain = lambda shape: 1.0 + nrm(shape, 0.02)
    return {
        'x_prompt': nrm((BATCH, SEQ, D_MODEL), 1.0),
        'x_sample': nrm((DEC_BATCH, DEC_SEQ, D_MODEL), 1.0),
        'cache_sb_k': nrm((N_EVEN, DEC_BATCH, PAST_LEN, H_B, DH_B), 1.0),
        'cache_sb_v': nrm((N_EVEN, DEC_BATCH, PAST_LEN, H_B, DH_B), 1.0),
        'state_hgrn': nrm((N_EVEN, DEC_BATCH, H_A, DK_A, DV_A), 0.5),
        'ffn1_norm': gain((DEPTH, D_MODEL)),
        'ffn1_w_gate': nrm((DEPTH, D_MODEL, D_FF), D_MODEL ** -0.5),
        'ffn1_w_up': nrm((DEPTH, D_MODEL, D_FF), D_MODEL ** -0.5),
        'ffn1_w_down': nrm((DEPTH, D_FF, D_MODEL), D_FF ** -0.5),
        'mix_norm': gain((DEPTH, D_MODEL)),
        'ffn2_norm': gain((DEPTH, D_MODEL)),
        'ffn2_w_gate': nrm((DEPTH, D_MODEL, D_FF), D_MODEL ** -0.5),
        'ffn2_w_up': nrm((DEPTH, D_MODEL, D_FF), D_MODEL ** -0.5),
        'ffn2_w_down': nrm((DEPTH, D_FF, D_MODEL), D_FF ** -0.5),
        'ab_w_in': nrm((N_EVEN, D_MODEL, D_IN_EVEN), D_MODEL ** -0.5),
        'ab_lb': nrm((N_EVEN + 1, H_A * DK_A), 0.1),
        'ab_g_out': gain((N_EVEN, H_A, DV_A)),
        'ab_g_q': gain((N_EVEN, DH_B)),
        'ab_g_k': gain((N_EVEN, DH_B)),
        'ab_w_out': nrm((N_EVEN, D_MIX_EVEN, D_MODEL), D_MIX_EVEN ** -0.5),
        'c_w_in': nrm((N_ODD, D_MODEL, 2 * D_C), D_MODEL ** -0.5),
        'c_ln_g': gain((N_ODD, D_C)),
        'c_ln_b': nrm((N_ODD, D_C), 0.02),
        'c_w_s': nrm((N_ODD, G_C, GMLP_LEN, GMLP_LEN), GMLP_LEN ** -0.5),
        'c_b_s': 1.0 + nrm((N_ODD, G_C, GMLP_LEN), 0.1),
        'c_w_out': nrm((N_ODD, D_C, D_MODEL), D_C ** -0.5),
    }


def reference(x_prompt, x_sample, cache_sb_k, cache_sb_v, state_hgrn, ffn1_norm, ffn1_w_gate, ffn1_w_up,
              ffn1_w_down, mix_norm, ffn2_norm, ffn2_w_gate, ffn2_w_up, ffn2_w_down, ab_w_in, ab_lb, ab_g_out,
              ab_g_q, ab_g_k, ab_w_out, c_w_in, c_ln_g, c_ln_b, c_w_s, c_b_s, c_w_out):
    p = {'ffn1_norm': ffn1_norm, 'ffn1_w_gate': ffn1_w_gate, 'ffn1_w_up': ffn1_w_up, 'ffn1_w_down': ffn1_w_down,
         'mix_norm': mix_norm, 'ffn2_norm': ffn2_norm, 'ffn2_w_gate': ffn2_w_gate, 'ffn2_w_up': ffn2_w_up,
         'ffn2_w_down': ffn2_w_down, 'ab_w_in': ab_w_in, 'ab_lb': ab_lb, 'ab_g_out': ab_g_out, 'ab_g_q': ab_g_q,
         'ab_g_k': ab_g_k, 'ab_w_out': ab_w_out, 'c_w_in': c_w_in, 'c_ln_g': c_ln_g, 'c_ln_b': c_ln_b,
         'c_w_s': c_w_s, 'c_b_s': c_b_s, 'c_w_out': c_w_out}
    y_prompt, pk, pv, ps, _ = _trunk(x_prompt, p, None, None, None)
    y_sample, sk, sv, ss, scv = _trunk(x_sample, p, state_hgrn, cache_sb_k, cache_sb_v)
    return (y_prompt, y_sample, jnp.stack(pk), jnp.stack(pv), jnp.stack(ps),
            jnp.stack(sk), jnp.stack(sv), jnp.stack(ss), jnp.stack(scv))
```

```python
import functools

import jax
import jax.numpy as jnp
from jax import lax
from jax.experimental import pallas as pl
from jax.experimental.pallas import tpu as pltpu

F32 = jnp.float32
BF16 = jnp.bfloat16

EPS = 1e-6
FFN_SCALE = 0.5
HEAD_DIM = 128
CHUNK = 64
GMLP_LEN = 128
GMLP_GROUPS = 8
V7X_VMEM_LIMIT_BYTES = 56 * 1024 * 1024
TOKEN_TILE = 512
HGRN_CHUNK = 128
SB_BLOCK = 256


def _params(*sem):
    return pltpu.CompilerParams(dimension_semantics=sem, vmem_limit_bytes=V7X_VMEM_LIMIT_BYTES)


def _resident(shape):
    zeros = (0,) * len(shape)
    return pl.BlockSpec(shape, lambda *_: zeros, pipeline_mode=pl.Buffered(1))


def _rows(tile, width):
    return pl.BlockSpec((tile, width), lambda i: (i, 0))


def _sigmoid(x):
    return 1.0 / (1.0 + jnp.exp(-x))


def _rms_rows(x, g):
    ms = jnp.mean(x * x, axis=-1, keepdims=True)
    return x * lax.rsqrt(ms + EPS) * g


def _dot(a, b):
    return jnp.dot(a, b, preferred_element_type=F32)


def _dot_nt(a, b):
    return lax.dot_general(a, b, (((1,), (1,)), ((), ())), preferred_element_type=F32)


def _dot_tn(a, b):
    return lax.dot_general(a, b, (((0,), (0,)), ((), ())), preferred_element_type=F32)


def _split_bf16(x):
    hi = x.astype(BF16)
    lo = (x - hi.astype(F32)).astype(BF16)
    return hi, lo


def _ffn_body(n_pre, has_next, *refs):
    x_ref = refs[0]
    pre = refs[1:1 + 2 * n_pre]
    g_ref, wg_ref, wu_ref, wd_ref = refs[1 + 2 * n_pre:5 + 2 * n_pre]
    pos = 5 + 2 * n_pre
    g2_ref = refs[pos] if has_next else None
    pos += int(has_next)
    xo_ref = refs[pos]
    ho_ref = refs[pos + 1] if has_next else None

    x = x_ref[...]
    for i in range(n_pre):
        x = x + _dot(pre[2 * i][...], pre[2 * i + 1][...])
    h = _rms_rows(x, g_ref[...]).astype(BF16)
    a = _dot(h, wg_ref[...])
    b = _dot(h, wu_ref[...])
    act = (a * _sigmoid(a) * b).astype(BF16)
    x = x + FFN_SCALE * _dot(act, wd_ref[...])
    xo_ref[...] = x
    if has_next:
        ho_ref[...] = _rms_rows(x, g2_ref[...]).astype(BF16)


def _ffn(x, pre, g, wg, wu, wd, g_next, tile):
    n, d = x.shape
    d_ff = wg.shape[1]
    has_next = g_next is not None
    in_specs = [_rows(tile, d)]
    args = [x]
    for m, w in pre:
        in_specs += [_rows(tile, m.shape[1]), _resident(w.shape)]
        args += [m, w]
    in_specs += [_resident((1, d)), _resident((d, d_ff)), _resident((d, d_ff)), _resident((d_ff, d))]
    args += [g.reshape(1, d), wg, wu, wd]
    out_shape = [jax.ShapeDtypeStruct((n, d), F32)]
    out_specs = [_rows(tile, d)]
    if has_next:
        in_specs.append(_resident((1, d)))
        args.append(g_next.reshape(1, d))
        out_shape.append(jax.ShapeDtypeStruct((n, d), BF16))
        out_specs.append(_rows(tile, d))
    outs = pl.pallas_call(
        functools.partial(_ffn_body, len(pre), has_next),
        out_shape=out_shape, grid=(n // tile,), in_specs=in_specs, out_specs=out_specs,
        compiler_params=_params("parallel"), name="ffn")(*args)
    return (outs[0], outs[1]) if has_next else (outs[0], None)


def _abin_body(layer, h_ref, w_ref, lbp_ref, gq_ref, gk_ref,
               qa_ref, lf_ref, va_ref, ga_ref, qb_ref, k_ref, v_ref):
    h = h_ref[...]
    d = h.shape[1]
    n_heads = d // HEAD_DIM

    def seg(i):
        return _dot(h, w_ref[:, i * d:(i + 1) * d])

    lbp = lbp_ref[...]
    e = jnp.exp(lbp - jnp.max(lbp, axis=0, keepdims=True))
    lb = jnp.sum(e[:layer + 1], axis=0, keepdims=True) / jnp.sum(e, axis=0, keepdims=True)

    qa = seg(0)
    qa_ref[...] = (qa * _sigmoid(qa)).astype(BF16)
    f = lb + (1.0 - lb) * _sigmoid(seg(1))
    lf_ref[...] = jnp.log(f)
    va_ref[...] = seg(2).astype(BF16)
    ga_ref[...] = _sigmoid(seg(3)).astype(BF16)

    def head_rms(t, g, out_ref, scale):
        for hh in range(n_heads):
            sl = slice(hh * HEAD_DIM, (hh + 1) * HEAD_DIM)
            out_ref[:, sl] = (_rms_rows(t[:, sl], g) * scale).astype(out_ref.dtype)

    head_rms(seg(4), gq_ref[...], qb_ref, HEAD_DIM ** -0.5)
    head_rms(seg(5), gk_ref[...], k_ref, 1.0)
    v_ref[...] = seg(6)


def _abin(h, w_in, lb_param, g_q, g_k, layer, tile):
    n, d = h.shape
    outs = [BF16, F32, BF16, BF16, BF16, F32, F32]
    return pl.pallas_call(
        functools.partial(_abin_body, layer),
        out_shape=[jax.ShapeDtypeStruct((n, d), t) for t in outs],
        grid=(n // tile,),
        in_specs=[_rows(tile, d), _resident(w_in.shape), _resident(lb_param.shape),
                  _resident((1, HEAD_DIM)), _resident((1, HEAD_DIM))],
        out_specs=[_rows(tile, d) for _ in outs],
        compiler_params=_params("parallel"), name="ab_in_proj")(
            h, w_in, lb_param, g_q.reshape(1, HEAD_DIM), g_k.reshape(1, HEAD_DIM))


def _hgrn_body(chunk, n_chunks, has_s0, *refs):
    q_ref, lf_ref, v_ref, gt_ref = refs[:4]
    pos = 4
    s0_ref = refs[pos] if has_s0 else None
    pos += int(has_s0)
    go_ref, o_ref, so_ref = refs[pos:pos + 3]

    row = lax.broadcasted_iota(jnp.int32, (chunk, chunk), 0)
    col = lax.broadcasted_iota(jnp.int32, (chunk, chunk), 1)
    causal = col <= row
    lower = jnp.where(causal, 1.0, 0.0).astype(BF16)
    g_out = go_ref[...]
    mid = chunk // 2

    def step(i, st):
        rows = pl.ds(pl.multiple_of(i * chunk, chunk), chunk)
        q = q_ref[rows, :].astype(F32)
        lf = lf_ref[rows, :]
        v = v_ref[rows, :]
        k = 1.0 - jnp.exp(lf)
        hi, lo = _split_bf16(lf)
        c2 = _dot(lower, jnp.concatenate([hi, lo], axis=1))
        c = c2[:, :HEAD_DIM] + c2[:, HEAD_DIM:]
        c_last = c[chunk - 1:chunk, :]
        c_mid = c[mid - 1:mid, :]
        o = _dot_nt((q * jnp.exp(c)).astype(BF16), st.astype(BF16))
        qe = (q * jnp.exp(c - c_mid)).astype(BF16)
        ke = (k * jnp.exp(c_mid - c)).astype(BF16)
        scores = jnp.where(causal, _dot_nt(qe, ke), 0.0)
        o = o + _dot(scores.astype(BF16), v)
        kd = (k * jnp.exp(c_last - c)).astype(BF16)
        st = st * jnp.exp(c_last) + _dot_tn(v, kd)
        o = _rms_rows(o, g_out) * gt_ref[rows, :].astype(F32)
        o_ref[rows, :] = o.astype(o_ref.dtype)
        return st

    st0 = s0_ref[...].T if has_s0 else jnp.zeros((HEAD_DIM, HEAD_DIM), F32)
    st = lax.fori_loop(0, n_chunks, step, st0) if n_chunks > 1 else step(0, st0)
    so_ref[...] = st.T


def _hgrn(qa, lf, va, ga, s0, g_out, batch, seq, chunk):
    n, d = qa.shape
    n_heads = d // HEAD_DIM
    has_s0 = s0 is not None
    tok = pl.BlockSpec((seq, HEAD_DIM), lambda b, h: (b, h))
    state = pl.BlockSpec((None, None, HEAD_DIM, HEAD_DIM), lambda b, h: (b, h, 0, 0))
    in_specs = [tok, tok, tok, tok]
    args = [qa, lf, va, ga]
    if has_s0:
        in_specs.append(state)
        args.append(s0)
    in_specs.append(pl.BlockSpec((None, 1, HEAD_DIM), lambda b, h: (h, 0, 0)))
    args.append(g_out.reshape(n_heads, 1, HEAD_DIM))
    return pl.pallas_call(
        functools.partial(_hgrn_body, chunk, seq // chunk, has_s0),
        out_shape=[jax.ShapeDtypeStruct((n, d), BF16),
                   jax.ShapeDtypeStruct((batch, n_heads, HEAD_DIM, HEAD_DIM), F32)],
        grid=(batch, n_heads), in_specs=in_specs, out_specs=[tok, state],
        compiler_params=_params("parallel", "parallel"), name="hgrn2")(*args)


def _sb_block(q, kb, vb, upper, mask, acc, run):
    z = _dot_nt(q, kb)
    sp = jnp.maximum(z, 0.0) + jnp.log(1.0 + jnp.exp(-jnp.abs(z)))
    log_stay = -sp if mask is None else jnp.where(mask, -sp, 0.0)
    hi, lo = _split_bf16(log_stay)
    suffix = _dot(hi, upper) + _dot(lo, upper)
    w = jnp.exp(z - sp + suffix + run)
    if mask is not None:
        w = jnp.where(mask, w, 0.0)
    acc = acc + _dot(w.astype(BF16), vb)
    run = run + jnp.sum(log_stay, axis=-1, keepdims=True)
    return acc, run


def _strict_upper(n):
    j = lax.broadcasted_iota(jnp.int32, (n, n), 0)
    k = lax.broadcasted_iota(jnp.int32, (n, n), 1)
    return jnp.where(j > k, 1.0, 0.0).astype(BF16)


def _sb_body(blk, n_q, past_blk, n_past, *refs):
    q_ref, k_ref, v_ref = refs[:3]
    pos = 3
    kp_ref = vp_ref = None
    if n_past:
        kp_ref, vp_ref = refs[3:5]
        pos = 5
    o_ref = refs[pos]

    t = lax.broadcasted_iota(jnp.int32, (blk, blk), 0)
    s = lax.broadcasted_iota(jnp.int32, (blk, blk), 1)
    diag_mask = s < t
    upper = _strict_upper(blk)
    upper_past = _strict_upper(past_blk) if n_past else None

    for qi in range(n_q):
        rows = pl.ds(qi * blk, blk)
        q = q_ref[rows, :]
        acc = jnp.zeros((blk, HEAD_DIM), F32)
        run = jnp.zeros((blk, 1), F32)
        acc, run = _sb_block(q, k_ref[rows, :].astype(BF16), v_ref[rows, :].astype(BF16),
                             upper, diag_mask, acc, run)

        def older(j, carry, qi=qi, q=q):
            kr = pl.ds(pl.multiple_of((qi - 1 - j) * blk, blk), blk)
            return _sb_block(q, k_ref[kr, :].astype(BF16), v_ref[kr, :].astype(BF16),
                             upper, None, *carry)

        if qi > 0:
            acc, run = lax.fori_loop(0, qi, older, (acc, run))

        def past(j, carry, q=q):
            kr = pl.ds(pl.multiple_of((n_past - 1 - j) * past_blk, past_blk), past_blk)
            return _sb_block(q, kp_ref[kr, :].astype(BF16), vp_ref[kr, :].astype(BF16),
                             upper_past, None, *carry)

        if n_past:
            acc, run = lax.fori_loop(0, n_past, past, (acc, run))
        o_ref[rows, :] = acc.astype(o_ref.dtype)


def _stick_breaking(qn, k, v, k_past, v_past, batch, seq):
    n, d = qn.shape
    n_heads = d // HEAD_DIM
    blk = min(seq, SB_BLOCK)
    tok = pl.BlockSpec((seq, HEAD_DIM), lambda b, h: (b, h))
    in_specs = [tok, tok, tok]
    args = [qn, k, v]
    past_blk = n_past = 0
    if k_past is not None:
        past_len = k_past.shape[1]
        past_blk = min(past_len, SB_BLOCK)
        n_past = past_len // past_blk
        past = pl.BlockSpec((None, past_len, HEAD_DIM), lambda b, h: (b, 0, h))
        in_specs += [past, past]
        args += [k_past.reshape(batch, past_len, d), v_past.reshape(batch, past_len, d)]
    return pl.pallas_call(
        functools.partial(_sb_body, blk, seq // blk, past_blk, n_past),
        out_shape=jax.ShapeDtypeStruct((n, d), BF16),
        grid=(batch, n_heads), in_specs=in_specs, out_specs=tok,
        compiler_params=_params("parallel", "parallel"), name="stick_breaking")(*args)


def _gelu_tanh(x):
    return 0.5 * x * (1.0 + jnp.tanh(0.7978845608028654 * (x + 0.044715 * (x * x * x))))


def _cin_body(h_ref, w_ref, g_ref, b_ref, u_ref, v_ref):
    y = _gelu_tanh(_dot(h_ref[...], w_ref[...]))
    d_c = y.shape[1] // 2
    u_ref[...] = y[:, :d_c].astype(BF16)
    v = y[:, d_c:]
    vc = v - jnp.mean(v, axis=-1, keepdims=True)
    var = jnp.mean(vc * vc, axis=-1, keepdims=True)
    v_ref[...] = vc * lax.rsqrt(var + EPS) * g_ref[...] + b_ref[...]


def _cin(h, w_in, ln_g, ln_b, tile):
    n, d = h.shape
    d_c = w_in.shape[1] // 2
    return pl.pallas_call(
        _cin_body,
        out_shape=[jax.ShapeDtypeStruct((n, d_c), BF16), jax.ShapeDtypeStruct((n, d_c), F32)],
        grid=(n // tile,),
        in_specs=[_rows(tile, d), _resident(w_in.shape), _resident((1, d_c)), _resident((1, d_c))],
        out_specs=[_rows(tile, d_c), _rows(tile, d_c)],
        compiler_params=_params("parallel"), name="gmlp_in_proj")(
            h, w_in, ln_g.reshape(1, d_c), ln_b.reshape(1, d_c))


def _cgate_body(length, n_sub, u_ref, v_ref, ws_ref, bs_ref, y_ref):
    i = lax.broadcasted_iota(jnp.int32, (length, length), 0)
    j = lax.broadcasted_iota(jnp.int32, (length, length), 1)
    mask = (j // CHUNK) <= (i // CHUNK)
    d_c = u_ref.shape[1]
    dg = d_c // GMLP_GROUPS
    bias = bs_ref[...]
    for g in range(GMLP_GROUPS):
        w = jnp.where(mask, ws_ref[g], 0.0).astype(BF16)
        cols = slice(g * dg, (g + 1) * dg)
        for c in range(n_sub):
            rows = slice(c * length, (c + 1) * length)
            mixed = _dot(w, v_ref[rows, cols].astype(BF16)) + bias[:, g:g + 1]
            y_ref[rows, cols] = (u_ref[rows, cols].astype(F32) * mixed).astype(BF16)


def _cgate(u, v, w_s, b_s, length, tile):
    n, d_c = u.shape
    ws = w_s[:, :length, :length]
    bs = b_s[:, :length].T
    return pl.pallas_call(
        functools.partial(_cgate_body, length, tile // length),
        out_shape=jax.ShapeDtypeStruct((n, d_c), BF16),
        grid=(n // tile,),
        in_specs=[_rows(tile, d_c), _rows(tile, d_c), _resident(ws.shape), _resident(bs.shape)],
        out_specs=_rows(tile, d_c),
        compiler_params=_params("parallel"), name="gmlp_gate")(u, v, ws, bs)


def _trunk(x3, p, hgrn_state, sb_k, sb_v):
    batch, seq, d = x3.shape
    n = batch * seq
    tile = min(TOKEN_TILE, n)
    depth = p['ffn1_norm'].shape[0]
    x = x3.reshape(n, d)
    new_k, new_v, new_s, new_cv = [], [], [], []
    pre = []
    for l in range(depth):
        x, h = _ffn(x, pre, p['ffn1_norm'][l], p['ffn1_w_gate'][l], p['ffn1_w_up'][l],
                    p['ffn1_w_down'][l], p['mix_norm'][l], tile)
        if l % 2 == 0:
            e = l // 2
            qa, lf, va, ga, qb, k, v = _abin(h, p['ab_w_in'][e], p['ab_lb'], p['ab_g_q'][e],
                                             p['ab_g_k'][e], e, tile)
            mix_a, s_new = _hgrn(qa, lf, va, ga, None if hgrn_state is None else hgrn_state[e],
                                 p['ab_g_out'][e], batch, seq, min(seq, HGRN_CHUNK))
            mix_b = _stick_breaking(qb, k, v, None if sb_k is None else sb_k[e],
                                    None if sb_v is None else sb_v[e], batch, seq)
            w_out = p['ab_w_out'][e]
            pre = [(mix_a, w_out[:d]), (mix_b, w_out[d:])]
            new_k.append(k.reshape(batch, seq, d // HEAD_DIM, HEAD_DIM))
            new_v.append(v.reshape(batch, seq, d // HEAD_DIM, HEAD_DIM))
            new_s.append(s_new)
        else:
            o = l // 2
            u, cv = _cin(h, p['c_w_in'][o], p['c_ln_g'][o], p['c_ln_b'][o], tile)
            y = _cgate(u, cv, p['c_w_s'][o], p['c_b_s'][o], min(seq, GMLP_LEN), tile)
            pre = [(y, p['c_w_out'][o])]
            new_cv.append(cv.reshape(batch, seq, -1))
        x, _ = _ffn(x, pre, p['ffn2_norm'][l], p['ffn2_w_gate'][l], p['ffn2_w_up'][l],
                    p['ffn2_w_down'][l], None, tile)
        pre = []
    return x.reshape(batch, seq, d), new_k, new_v, new_s, new_cv


_MATMUL_WEIGHTS = ('ffn1_w_gate', 'ffn1_w_up', 'ffn1_w_down', 'ffn2_w_gate', 'ffn2_w_up',
                   'ffn2_w_down', 'ab_w_in', 'ab_w_out', 'c_w_in', 'c_w_out')


def kernel(x_prompt, x_sample, cache_sb_k, cache_sb_v, state_hgrn, ffn1_norm, ffn1_w_gate, ffn1_w_up, ffn1_w_down, mix_norm, ffn2_norm, ffn2_w_gate, ffn2_w_up, ffn2_w_down, ab_w_in, ab_lb, ab_g_out, ab_g_q, ab_g_k, ab_w_out, c_w_in, c_ln_g, c_ln_b, c_w_s, c_b_s, c_w_out):
    p = {'ffn1_norm': ffn1_norm, 'ffn1_w_gate': ffn1_w_gate, 'ffn1_w_up': ffn1_w_up, 'ffn1_w_down': ffn1_w_down,
         'mix_norm': mix_norm, 'ffn2_norm': ffn2_norm, 'ffn2_w_gate': ffn2_w_gate, 'ffn2_w_up': ffn2_w_up,
         'ffn2_w_down': ffn2_w_down, 'ab_w_in': ab_w_in, 'ab_lb': ab_lb, 'ab_g_out': ab_g_out, 'ab_g_q': ab_g_q,
         'ab_g_k': ab_g_k, 'ab_w_out': ab_w_out, 'c_w_in': c_w_in, 'c_ln_g': c_ln_g, 'c_ln_b': c_ln_b,
         'c_w_s': c_w_s, 'c_b_s': c_b_s, 'c_w_out': c_w_out}
    for name in _MATMUL_WEIGHTS:
        p[name] = p[name].astype(BF16)
    y_prompt, pk, pv, ps, _ = _trunk(x_prompt, p, None, None, None)
    y_sample, sk, sv, ss, scv = _trunk(x_sample, p, state_hgrn, cache_sb_k, cache_sb_v)
    return (y_prompt, y_sample, jnp.stack(pk), jnp.stack(pv), jnp.stack(ps),
            jnp.stack(sk), jnp.stack(sv), jnp.stack(ss), jnp.stack(scv))
```

```python
import functools

import jax
import jax.numpy as jnp
from jax import lax
from jax.experimental import pallas as pl
from jax.experimental.pallas import tpu as pltpu

F32 = jnp.float32
BF16 = jnp.bfloat16

EPS = 1e-6
FFN_SCALE = 0.5
HEAD_DIM = 128
N_HEADS = 8
CHUNK = 64
GMLP_LEN = 128
GMLP_GROUPS = 8
V7X_VMEM_LIMIT_BYTES = 56 * 1024 * 1024
TOKEN_TILE = 512
HGRN_CHUNK = 128
HGRN_HEADS_PER_STEP = 4
HGRN_SAFE_EXPONENT = 80.0
SB_BLOCK = 256
SB_HEADS_PER_STEP = 2
SB_EXIT = -105.0


def _params(*sem):
    return pltpu.CompilerParams(dimension_semantics=sem, vmem_limit_bytes=V7X_VMEM_LIMIT_BYTES)


def _resident(shape):
    zeros = (0,) * len(shape)
    return pl.BlockSpec(shape, lambda *_: zeros, pipeline_mode=pl.Buffered(1))


def _rows(tile, width):
    return pl.BlockSpec((tile, width), lambda i: (i, 0))


def _sigmoid(x):
    return 1.0 / (1.0 + jnp.exp(-x))


def _rms_rows(x, g):
    ms = jnp.mean(x * x, axis=-1, keepdims=True)
    return x * lax.rsqrt(ms + EPS) * g


def _dot(a, b):
    return jnp.dot(a, b, preferred_element_type=F32)


def _dot_nt(a, b):
    return lax.dot_general(a, b, (((1,), (1,)), ((), ())), preferred_element_type=F32)


def _dot_tn(a, b):
    return lax.dot_general(a, b, (((0,), (0,)), ((), ())), preferred_element_type=F32)


def _split_bf16(x):
    hi = x.astype(BF16)
    lo = (x - hi.astype(F32)).astype(BF16)
    return hi, lo


def _head_rows(head, n):
    return pl.ds(head, n, stride=N_HEADS)


def _ffn_body(n_pre, has_next, *refs):
    x_ref = refs[0]
    pre = refs[1:1 + 2 * n_pre]
    g_ref, wg_ref, wu_ref, wd_ref = refs[1 + 2 * n_pre:5 + 2 * n_pre]
    pos = 5 + 2 * n_pre
    g2_ref = refs[pos] if has_next else None
    pos += int(has_next)
    xo_ref = refs[pos]
    ho_ref = refs[pos + 1] if has_next else None

    x = x_ref[...]
    for i in range(n_pre):
        x = x + _dot(pre[2 * i][...], pre[2 * i + 1][...])
    h = _rms_rows(x, g_ref[...]).astype(BF16)
    a = _dot(h, wg_ref[...])
    b = _dot(h, wu_ref[...])
    act = (a * _sigmoid(a) * b).astype(BF16)
    x = x + FFN_SCALE * _dot(act, wd_ref[...])
    xo_ref[...] = x
    if has_next:
        ho_ref[...] = _rms_rows(x, g2_ref[...]).astype(BF16)


def _ffn(x, pre, g, wg, wu, wd, g_next, tile):
    n, d = x.shape
    d_ff = wg.shape[1]
    has_next = g_next is not None
    in_specs = [_rows(tile, d)]
    args = [x]
    for m, w in pre:
        in_specs += [_rows(tile, m.shape[1]), _resident(w.shape)]
        args += [m, w]
    in_specs += [_resident((1, d)), _resident((d, d_ff)), _resident((d, d_ff)), _resident((d_ff, d))]
    args += [g.reshape(1, d), wg, wu, wd]
    out_shape = [jax.ShapeDtypeStruct((n, d), F32)]
    out_specs = [_rows(tile, d)]
    if has_next:
        in_specs.append(_resident((1, d)))
        args.append(g_next.reshape(1, d))
        out_shape.append(jax.ShapeDtypeStruct((n, d), BF16))
        out_specs.append(_rows(tile, d))
    outs = pl.pallas_call(
        functools.partial(_ffn_body, len(pre), has_next),
        out_shape=out_shape, grid=(n // tile,), in_specs=in_specs, out_specs=out_specs,
        compiler_params=_params("parallel"), name="ffn")(*args)
    return (outs[0], outs[1]) if has_next else (outs[0], None)


def _abin_body(layer, h_ref, w_ref, lbp_ref, gq_ref, gk_ref,
               qa_ref, lf_ref, va_ref, ga_ref, qb_ref, kb_ref, vb_ref, k_ref, v_ref):
    h = h_ref[...]
    tile, d = h.shape

    def seg(i):
        return _dot(h, w_ref[:, i * d:(i + 1) * d])

    lbp = lbp_ref[...]
    e = jnp.exp(lbp - jnp.max(lbp, axis=0, keepdims=True))
    lb = jnp.sum(e[:layer + 1], axis=0, keepdims=True) / jnp.sum(e, axis=0, keepdims=True)

    qa = seg(0)
    qa_ref[...] = (qa * _sigmoid(qa)).astype(BF16)
    f = lb + (1.0 - lb) * _sigmoid(seg(1))
    lf_ref[...] = jnp.log(f)
    va_ref[...] = seg(2).astype(BF16)
    ga_ref[...] = _sigmoid(seg(3)).astype(BF16)

    qb = seg(4)
    kb = seg(5)
    vb = seg(6)
    vb_ref[...] = vb.astype(BF16)
    for hh in range(N_HEADS):
        sl = slice(hh * HEAD_DIM, (hh + 1) * HEAD_DIM)
        qb_ref[:, sl] = (_rms_rows(qb[:, sl], gq_ref[...]) * HEAD_DIM ** -0.5).astype(BF16)
        kn = _rms_rows(kb[:, sl], gk_ref[...])
        kb_ref[:, sl] = kn.astype(BF16)
        k_ref[_head_rows(hh, tile), :] = kn
        v_ref[_head_rows(hh, tile), :] = vb[:, sl]


def _abin(h, w_in, lb_param, g_q, g_k, layer, tile):
    n, d = h.shape
    assert d == N_HEADS * HEAD_DIM
    feat = [BF16, F32, BF16, BF16, BF16, BF16, BF16]
    out_shape = [jax.ShapeDtypeStruct((n, d), t) for t in feat]
    out_specs = [_rows(tile, d) for _ in feat]
    for _ in range(2):
        out_shape.append(jax.ShapeDtypeStruct((n * N_HEADS, HEAD_DIM), F32))
        out_specs.append(_rows(tile * N_HEADS, HEAD_DIM))
    return pl.pallas_call(
        functools.partial(_abin_body, layer),
        out_shape=out_shape, grid=(n // tile,),
        in_specs=[_rows(tile, d), _resident(w_in.shape), _resident(lb_param.shape),
                  _resident((1, HEAD_DIM)), _resident((1, HEAD_DIM))],
        out_specs=out_specs,
        compiler_params=_params("parallel"), name="ab_in_proj")(
            h, w_in, lb_param, g_q.reshape(1, HEAD_DIM), g_k.reshape(1, HEAD_DIM))


def _hgrn_scores(q, k, lf2, c, chunk, factored):
    row = lax.broadcasted_iota(jnp.int32, (chunk, chunk), 0)
    col = lax.broadcasted_iota(jnp.int32, (chunk, chunk), 1)
    if factored:
        mid = chunk // 2
        c_mid = c[mid - 1:mid, :]
        qe = (q * jnp.exp(c - c_mid)).astype(BF16)
        ke = (k * jnp.exp(c_mid - c)).astype(BF16)
        return jnp.where(col <= row, _dot_nt(qe, ke), 0.0)
    t1 = lax.broadcasted_iota(jnp.int32, (chunk, 1), 0)
    scores = jnp.where(row == col, jnp.sum(q * k, axis=-1, keepdims=True), 0.0)
    half = chunk // 2
    while half >= 1:
        blk = 2 * half
        same = (row // blk) == (col // blk)
        mid = (row // blk) * blk + half
        right = row >= mid
        seg = same & ((right & (col >= mid) & (col <= row)) | ((col > row) & (col < mid)))
        d2 = _dot(jnp.where(seg, 1.0, 0.0).astype(BF16), lf2)
        decay = jnp.exp(d2[:, :HEAD_DIM] + d2[:, HEAD_DIM:])
        right1 = (t1 % blk) >= half
        qe = jnp.where(right1, q * decay, 0.0).astype(BF16)
        ke = jnp.where(right1, 0.0, k * decay).astype(BF16)
        scores = scores + jnp.where(same, _dot_nt(qe, ke), 0.0)
        half //= 2
    return scores


def _hgrn_body(chunk, n_chunks, heads, has_s0, factored, *refs):
    q_ref, lf_ref, v_ref, gt_ref = refs[:4]
    pos = 4
    s0_ref = refs[pos] if has_s0 else None
    pos += int(has_s0)
    go_ref, o_ref, so_ref = refs[pos:pos + 3]

    row = lax.broadcasted_iota(jnp.int32, (chunk, chunk), 0)
    col = lax.broadcasted_iota(jnp.int32, (chunk, chunk), 1)
    lower = jnp.where(col <= row, 1.0, 0.0).astype(BF16)

    head_cols = [slice(j * HEAD_DIM, (j + 1) * HEAD_DIM) for j in range(heads)]

    def step(i, states):
        rows = pl.ds(pl.multiple_of(i * chunk, chunk), chunk)
        lfs = [lf_ref[rows, cl] for cl in head_cols]
        lf2s = [jnp.concatenate(_split_bf16(lf), axis=1) for lf in lfs]
        c2s = [_dot(lower, lf2) for lf2 in lf2s]
        qs = [q_ref[rows, cl].astype(F32) for cl in head_cols]
        ks = [1.0 - jnp.exp(lf) for lf in lfs]
        cs = [c2[:, :HEAD_DIM] + c2[:, HEAD_DIM:] for c2 in c2s]
        c_lasts = [c[chunk - 1:chunk, :] for c in cs]
        inter = [_dot_nt((q * jnp.exp(c)).astype(BF16), st.astype(BF16)) for q, c, st in zip(qs, cs, states)]
        scores = [_hgrn_scores(q, k, lf2, c, chunk, factored) for q, k, lf2, c in zip(qs, ks, lf2s, cs)]
        kds = [(k * jnp.exp(cl - c)).astype(BF16) for k, cl, c in zip(ks, c_lasts, cs)]
        vs = [v_ref[rows, cl] for cl in head_cols]
        grown = [_dot_tn(v, kd) for v, kd in zip(vs, kds)]
        intra = [_dot(s.astype(BF16), v) for s, v in zip(scores, vs)]
        for j, cl in enumerate(head_cols):
            o = _rms_rows(inter[j] + intra[j], go_ref[j]) * gt_ref[rows, cl].astype(F32)
            o_ref[rows, cl] = o.astype(o_ref.dtype)
        return tuple(st * jnp.exp(cl) + g for st, cl, g in zip(states, c_lasts, grown))

    if has_s0:
        states = tuple(s0_ref[j].T for j in range(heads))
    else:
        states = tuple(jnp.zeros((HEAD_DIM, HEAD_DIM), F32) for _ in range(heads))
    states = lax.fori_loop(0, n_chunks, step, states) if n_chunks > 1 else step(0, states)
    for j in range(heads):
        so_ref[j] = states[j].T


def _hgrn_call(qa, lf, va, ga, s0, g_out, batch, seq, chunk, heads, factored):
    n, d = qa.shape
    has_s0 = s0 is not None
    width = heads * HEAD_DIM
    tok = pl.BlockSpec((seq, width), lambda b, h: (b, h))
    state = pl.BlockSpec((None, heads, HEAD_DIM, HEAD_DIM), lambda b, h: (b, h, 0, 0))
    in_specs = [tok, tok, tok, tok]
    args = [qa, lf, va, ga]
    if has_s0:
        in_specs.append(state)
        args.append(s0)
    in_specs.append(pl.BlockSpec((heads, 1, HEAD_DIM), lambda b, h: (h, 0, 0)))
    args.append(g_out.reshape(N_HEADS, 1, HEAD_DIM))
    return pl.pallas_call(
        functools.partial(_hgrn_body, chunk, seq // chunk, heads, has_s0, factored),
        out_shape=[jax.ShapeDtypeStruct((n, d), BF16),
                   jax.ShapeDtypeStruct((batch, N_HEADS, HEAD_DIM, HEAD_DIM), F32)],
        grid=(batch, N_HEADS // heads), in_specs=in_specs, out_specs=[tok, state],
        compiler_params=_params("parallel", "parallel"), name="hgrn2")(*args)


def _hgrn(qa, lf, va, ga, s0, g_out, lb_param, layer, batch, seq):
    chunk = min(seq, HGRN_CHUNK)
    heads = N_HEADS if seq < HGRN_CHUNK else HGRN_HEADS_PER_STEP
    call = functools.partial(_hgrn_call, qa, lf, va, ga, s0, g_out, batch, seq, chunk, heads)
    lb = jnp.cumsum(jax.nn.softmax(lb_param.astype(F32), axis=0), axis=0)[layer]
    reach = (chunk // 2) * jnp.max(-jnp.log(lb))
    return lax.cond(reach <= HGRN_SAFE_EXPONENT, lambda: call(True), lambda: call(False))


def _sb_logits_many(items):
    zs = [_dot_nt(q, kb) for q, kb, _, _ in items]
    sps = [jnp.maximum(z, 0.0) + jnp.log(1.0 + jnp.exp(-jnp.abs(z))) for z in zs]
    stays = [-sp if it[3] is None else jnp.where(it[3], -sp, 0.0) for sp, it in zip(sps, items)]
    splits = [_split_bf16(s) for s in stays]
    suffix = [_dot(hi, it[2]) + _dot(lo, it[2]) for (hi, lo), it in zip(splits, items)]
    return [(z - sp + sf, jnp.sum(s, axis=-1, keepdims=True)) for z, sp, sf, s in zip(zs, sps, suffix, stays)]


def _sb_accumulate(logits, stay, vb, mask, acc, run):
    w = jnp.exp(logits + run)
    if mask is not None:
        w = jnp.where(mask, w, 0.0)
    return acc + _dot(w.astype(BF16), vb), run + stay


def _strict_upper(n):
    j = lax.broadcasted_iota(jnp.int32, (n, n), 0)
    k = lax.broadcasted_iota(jnp.int32, (n, n), 1)
    return jnp.where(j > k, 1.0, 0.0).astype(BF16)


def _sb_body(blk, n_q, heads, past_blk, n_past, *refs):
    q_ref, k_ref, v_ref = refs[:3]
    if n_past:
        kp_new_ref, vp_new_ref, kp_hbm, vp_hbm, o_ref, kbuf, vbuf, sem = refs[3:]
    else:
        o_ref = refs[3]

    t = lax.broadcasted_iota(jnp.int32, (blk, blk), 0)
    s = lax.broadcasted_iota(jnp.int32, (blk, blk), 1)
    diag_mask = s < t
    upper = _strict_upper(blk)
    upper_past = _strict_upper(past_blk) if n_past else None
    head_cols = [slice(j * HEAD_DIM, (j + 1) * HEAD_DIM) for j in range(heads)]

    def still_live(runs):
        return functools.reduce(jnp.maximum, [jnp.max(r) for r in runs]) > SB_EXIT

    for qi in range(n_q):
        rows = pl.ds(qi * blk, blk)
        qs = [q_ref[rows, c] for c in head_cols]
        n_older = qi + n_past
        items = [(qs[j], k_ref[rows, c], upper, diag_mask) for j, c in enumerate(head_cols)]
        values = [(v_ref[rows, c], diag_mask) for c in head_cols]
        if qi > 0:
            prev = pl.ds((qi - 1) * blk, blk)
            items += [(qs[j], k_ref[prev, c], upper, None) for j, c in enumerate(head_cols)]
            values += [(v_ref[prev, c], None) for c in head_cols]
        elif n_past:
            items += [(qs[j], kp_new_ref[_head_rows(j, past_blk), :].astype(BF16), upper_past, None)
                      for j in range(heads)]
            values += [(vp_new_ref[_head_rows(j, past_blk), :].astype(BF16), None) for j in range(heads)]
        logits = _sb_logits_many(items)
        accs = [jnp.zeros((blk, HEAD_DIM), F32) for _ in range(heads)]
        runs = [jnp.zeros((blk, 1), F32) for _ in range(heads)]
        for n, ((lg, stay), (vb, mask)) in enumerate(zip(logits, values)):
            j = n % heads
            accs[j], runs[j] = _sb_accumulate(lg, stay, vb, mask, accs[j], runs[j])

        if n_older > 1:
            def cond(carry):
                return (carry[0] < n_older) & carry[1]

            def body(carry, qi=qi, qs=qs):
                idx = carry[0]
                accs, runs = list(carry[2]), list(carry[3])
                if n_past:
                    src = pl.ds(pl.multiple_of((n_past - 1 - idx) * past_blk * N_HEADS, past_blk * N_HEADS),
                                past_blk * N_HEADS)
                    b = pl.program_id(0)
                    copies = [pltpu.make_async_copy(kp_hbm.at[b, src, :], kbuf, sem.at[0]),
                              pltpu.make_async_copy(vp_hbm.at[b, src, :], vbuf, sem.at[1])]
                    for cp in copies:
                        cp.start()
                    for cp in copies:
                        cp.wait()
                if n_past:
                    items = [(qs[j], kbuf[_head_rows(j, past_blk), :].astype(BF16), upper_past, None)
                             for j in range(heads)]
                    vbs = [vbuf[_head_rows(j, past_blk), :].astype(BF16) for j in range(heads)]
                else:
                    kr = pl.ds(pl.multiple_of((qi - 1 - idx) * blk, blk), blk)
                    items = [(qs[j], k_ref[kr, c], upper, None) for j, c in enumerate(head_cols)]
                    vbs = [v_ref[kr, c] for c in head_cols]
                for j, (lg, stay) in enumerate(_sb_logits_many(items)):
                    accs[j], runs[j] = _sb_accumulate(lg, stay, vbs[j], None, accs[j], runs[j])
                return idx + 1, still_live(runs), tuple(accs), tuple(runs)

            _, _, accs, _ = lax.while_loop(cond, body, (jnp.int32(1), still_live(runs), tuple(accs), tuple(runs)))
        for j, c in enumerate(head_cols):
            o_ref[rows, c] = accs[j].astype(o_ref.dtype)


def _stick_breaking(qn, kb, vb, k_past, v_past, batch, seq):
    n, d = qn.shape
    blk = min(seq, SB_BLOCK)
    n_q = seq // blk
    heads = N_HEADS if seq < SB_BLOCK else SB_HEADS_PER_STEP
    tok = pl.BlockSpec((seq, heads * HEAD_DIM), lambda b, h: (b, h))
    in_specs = [tok, tok, tok]
    args = [qn, kb, vb]
    scratch = []
    past_blk = n_past = 0
    if k_past is not None:
        assert n_q == 1 and heads == N_HEADS, "cached keys are only supported for a single query block"
        past_len = k_past.shape[1]
        past_blk = min(past_len, SB_BLOCK)
        n_past = past_len // past_blk
        slab = past_blk * N_HEADS
        newest = pl.BlockSpec((None, slab, HEAD_DIM), lambda b, h: (b, n_past - 1, 0))
        hbm = pl.BlockSpec(memory_space=pl.ANY)
        in_specs += [newest, newest, hbm, hbm]
        kp = k_past.reshape(batch, past_len * N_HEADS, HEAD_DIM)
        vp = v_past.reshape(batch, past_len * N_HEADS, HEAD_DIM)
        args += [kp, vp, kp, vp]
        scratch = [pltpu.VMEM((slab, HEAD_DIM), F32), pltpu.VMEM((slab, HEAD_DIM), F32),
                   pltpu.SemaphoreType.DMA((2,))]
    return pl.pallas_call(
        functools.partial(_sb_body, blk, n_q, heads, past_blk, n_past),
        out_shape=jax.ShapeDtypeStruct((n, d), BF16),
        grid=(batch, N_HEADS // heads), in_specs=in_specs, out_specs=tok,
        scratch_shapes=scratch,
        compiler_params=_params("parallel", "parallel"), name="stick_breaking")(*args)


def _gelu_tanh(x):
    return 0.5 * x * (1.0 + jnp.tanh(0.7978845608028654 * (x + 0.044715 * (x * x * x))))


def _cin_body(h_ref, w_ref, g_ref, b_ref, u_ref, v_ref):
    y = _gelu_tanh(_dot(h_ref[...], w_ref[...]))
    d_c = y.shape[1] // 2
    u_ref[...] = y[:, :d_c].astype(BF16)
    v = y[:, d_c:]
    vc = v - jnp.mean(v, axis=-1, keepdims=True)
    var = jnp.mean(vc * vc, axis=-1, keepdims=True)
    v_ref[...] = vc * lax.rsqrt(var + EPS) * g_ref[...] + b_ref[...]


def _cin(h, w_in, ln_g, ln_b, tile):
    n, d = h.shape
    d_c = w_in.shape[1] // 2
    return pl.pallas_call(
        _cin_body,
        out_shape=[jax.ShapeDtypeStruct((n, d_c), BF16), jax.ShapeDtypeStruct((n, d_c), F32)],
        grid=(n // tile,),
        in_specs=[_rows(tile, d), _resident(w_in.shape), _resident((1, d_c)), _resident((1, d_c))],
        out_specs=[_rows(tile, d_c), _rows(tile, d_c)],
        compiler_params=_params("parallel"), name="gmlp_in_proj")(
            h, w_in, ln_g.reshape(1, d_c), ln_b.reshape(1, d_c))


def _cgate_body(length, n_sub, u_ref, v_ref, ws_ref, bs_ref, y_ref):
    i = lax.broadcasted_iota(jnp.int32, (length, length), 0)
    j = lax.broadcasted_iota(jnp.int32, (length, length), 1)
    mask = (j // CHUNK) <= (i // CHUNK)
    d_c = u_ref.shape[1]
    dg = d_c // GMLP_GROUPS
    bias = bs_ref[...]
    for g in range(GMLP_GROUPS):
        w = jnp.where(mask, ws_ref[g], 0.0).astype(BF16)
        cols = slice(g * dg, (g + 1) * dg)
        for c in range(n_sub):
            rows = slice(c * length, (c + 1) * length)
            mixed = _dot(w, v_ref[rows, cols].astype(BF16)) + bias[:, g:g + 1]
            y_ref[rows, cols] = (u_ref[rows, cols].astype(F32) * mixed).astype(BF16)


def _cgate(u, v, w_s, b_s, length, tile):
    n, d_c = u.shape
    ws = w_s[:, :length, :length]
    bs = b_s[:, :length].T
    return pl.pallas_call(
        functools.partial(_cgate_body, length, tile // length),
        out_shape=jax.ShapeDtypeStruct((n, d_c), BF16),
        grid=(n // tile,),
        in_specs=[_rows(tile, d_c), _rows(tile, d_c), _resident(ws.shape), _resident(bs.shape)],
        out_specs=_rows(tile, d_c),
        compiler_params=_params("parallel"), name="gmlp_gate")(u, v, ws, bs)


def _trunk(x3, p, hgrn_state, sb_k, sb_v):
    batch, seq, d = x3.shape
    n = batch * seq
    tile = min(TOKEN_TILE, n)
    depth = p['ffn1_norm'].shape[0]
    x = x3.reshape(n, d)
    new_k, new_v, new_s, new_cv = [], [], [], []
    pre = []
    for l in range(depth):
        x, h = _ffn(x, pre, p['ffn1_norm'][l], p['ffn1_w_gate'][l], p['ffn1_w_up'][l],
                    p['ffn1_w_down'][l], p['mix_norm'][l], tile)
        if l % 2 == 0:
            e = l // 2
            qa, lf, va, ga, qb, kb, vb, k, v = _abin(h, p['ab_w_in'][e], p['ab_lb'], p['ab_g_q'][e],
                                                     p['ab_g_k'][e], e, tile)
            mix_a, s_new = _hgrn(qa, lf, va, ga, None if hgrn_state is None else hgrn_state[e],
                                 p['ab_g_out'][e], p['ab_lb'], e, batch, seq)
            mix_b = _stick_breaking(qb, kb, vb, None if sb_k is None else sb_k[e],
                                    None if sb_v is None else sb_v[e], batch, seq)
            w_out = p['ab_w_out'][e]
            pre = [(mix_a, w_out[:d]), (mix_b, w_out[d:])]
            new_k.append(k.reshape(batch, seq, N_HEADS, HEAD_DIM))
            new_v.append(v.reshape(batch, seq, N_HEADS, HEAD_DIM))
            new_s.append(s_new)
        else:
            o = l // 2
            u, cv = _cin(h, p['c_w_in'][o], p['c_ln_g'][o], p['c_ln_b'][o], tile)
            y = _cgate(u, cv, p['c_w_s'][o], p['c_b_s'][o], min(seq, GMLP_LEN), tile)
            pre = [(y, p['c_w_out'][o])]
            new_cv.append(cv.reshape(batch, seq, -1))
        x, _ = _ffn(x, pre, p['ffn2_norm'][l], p['ffn2_w_gate'][l], p['ffn2_w_up'][l],
                    p['ffn2_w_down'][l], None, tile)
        pre = []
    return x.reshape(batch, seq, d), new_k, new_v, new_s, new_cv


_MATMUL_WEIGHTS = ('ffn1_w_gate', 'ffn1_w_up', 'ffn1_w_down', 'ffn2_w_gate', 'ffn2_w_up',
                   'ffn2_w_down', 'ab_w_in', 'ab_w_out', 'c_w_in', 'c_w_out')


def kernel(x_prompt, x_sample, cache_sb_k, cache_sb_v, state_hgrn, ffn1_norm, ffn1_w_gate, ffn1_w_up, ffn1_w_down, mix_norm, ffn2_norm, ffn2_w_gate, ffn2_w_up, ffn2_w_down, ab_w_in, ab_lb, ab_g_out, ab_g_q, ab_g_k, ab_w_out, c_w_in, c_ln_g, c_ln_b, c_w_s, c_b_s, c_w_out):
    p = {'ffn1_norm': ffn1_norm, 'ffn1_w_gate': ffn1_w_gate, 'ffn1_w_up': ffn1_w_up, 'ffn1_w_down': ffn1_w_down,
         'mix_norm': mix_norm, 'ffn2_norm': ffn2_norm, 'ffn2_w_gate': ffn2_w_gate, 'ffn2_w_up': ffn2_w_up,
         'ffn2_w_down': ffn2_w_down, 'ab_w_in': ab_w_in, 'ab_lb': ab_lb, 'ab_g_out': ab_g_out, 'ab_g_q': ab_g_q,
         'ab_g_k': ab_g_k, 'ab_w_out': ab_w_out, 'c_w_in': c_w_in, 'c_ln_g': c_ln_g, 'c_ln_b': c_ln_b,
         'c_w_s': c_w_s, 'c_b_s': c_b_s, 'c_w_out': c_w_out}
    for name in _MATMUL_WEIGHTS:
        p[name] = p[name].astype(BF16)
    y_prompt, pk, pv, ps, _ = _trunk(x_prompt, p, None, None, None)
    y_sample, sk, sv, ss, scv = _trunk(x_sample, p, state_hgrn, cache_sb_k, cache_sb_v)
    return (y_prompt, y_sample, jnp.stack(pk), jnp.stack(pv), jnp.stack(ps),
            jnp.stack(sk), jnp.stack(sv), jnp.stack(ss), jnp.stack(scv))
```

```python
import functools

import jax
import jax.numpy as jnp
from jax import lax
from jax.experimental import pallas as pl
from jax.experimental.pallas import tpu as pltpu

F32 = jnp.float32
BF16 = jnp.bfloat16

EPS = 1e-6
FFN_SCALE = 0.5
HEAD_DIM = 128
N_HEADS = 8
CHUNK = 64
GMLP_LEN = 128
GMLP_GROUPS = 8
V7X_VMEM_LIMIT_BYTES = 56 * 1024 * 1024
TOKEN_TILE = 512
HGRN_CHUNK = 128
HGRN_HEADS_PER_STEP = 4
HGRN_SAFE_EXPONENT = 60.0
SB_BLOCK = 256
SB_HEADS_PER_STEP = 2
SB_EXIT = -105.0


def _params(*sem):
    return pltpu.CompilerParams(dimension_semantics=sem, vmem_limit_bytes=V7X_VMEM_LIMIT_BYTES)


def _resident(shape):
    zeros = (0,) * len(shape)
    return pl.BlockSpec(shape, lambda *_: zeros, pipeline_mode=pl.Buffered(1))


def _layer(stacked, layer, row_block=0, n_row_blocks=1):
    rows = stacked.shape[1] // n_row_blocks
    block = (None, rows) + stacked.shape[2:]
    index = (layer, row_block) + (0,) * (stacked.ndim - 2)
    return stacked, pl.BlockSpec(block, lambda *_: index, pipeline_mode=pl.Buffered(1))


def _rows(tile, width):
    return pl.BlockSpec((tile, width), lambda i: (i, 0))


def _sigmoid(x):
    return 1.0 / (1.0 + jnp.exp(-x))


def _rms_rows(x, g):
    ms = jnp.mean(x * x, axis=-1, keepdims=True)
    return x * lax.rsqrt(ms + EPS) * g


def _dot(a, b):
    return jnp.dot(a, b, preferred_element_type=F32)


def _dot_nt(a, b):
    return lax.dot_general(a, b, (((1,), (1,)), ((), ())), preferred_element_type=F32)


def _dot_tn(a, b):
    return lax.dot_general(a, b, (((0,), (0,)), ((), ())), preferred_element_type=F32)


def _split_bf16(x):
    hi = x.astype(BF16)
    lo = (x - hi.astype(F32)).astype(BF16)
    return hi, lo


def _head_rows(head, n):
    return pl.ds(head, n, stride=N_HEADS)


def _ffn_body(n_pre, has_next, *refs):
    x_ref = refs[0]
    pre = refs[1:1 + 2 * n_pre]
    g_ref, wg_ref, wu_ref, wd_ref = refs[1 + 2 * n_pre:5 + 2 * n_pre]
    pos = 5 + 2 * n_pre
    g2_ref = refs[pos] if has_next else None
    pos += int(has_next)
    xo_ref = refs[pos]
    ho_ref = refs[pos + 1] if has_next else None

    x = x_ref[...]
    for i in range(n_pre):
        x = x + _dot(pre[2 * i][...], pre[2 * i + 1][...])
    h = _rms_rows(x, g_ref[...]).astype(BF16)
    a = _dot(h, wg_ref[...])
    b = _dot(h, wu_ref[...])
    act = (a * _sigmoid(a) * b).astype(BF16)
    x = x + FFN_SCALE * _dot(act, wd_ref[...])
    xo_ref[...] = x
    if has_next:
        ho_ref[...] = _rms_rows(x, g2_ref[...]).astype(BF16)


def _ffn(x, pre, g, wg, wu, wd, g_next, tile):
    n, d = x.shape
    has_next = g_next is not None
    in_specs = [_rows(tile, d)]
    args = [x]
    for m, (w, w_spec) in pre:
        in_specs += [_rows(tile, m.shape[1]), w_spec]
        args += [m, w]
    for arr, spec in (g, wg, wu, wd):
        in_specs.append(spec)
        args.append(arr)
    out_shape = [jax.ShapeDtypeStruct((n, d), F32)]
    out_specs = [_rows(tile, d)]
    if has_next:
        in_specs.append(g_next[1])
        args.append(g_next[0])
        out_shape.append(jax.ShapeDtypeStruct((n, d), BF16))
        out_specs.append(_rows(tile, d))
    outs = pl.pallas_call(
        functools.partial(_ffn_body, len(pre), has_next),
        out_shape=out_shape, grid=(n // tile,), in_specs=in_specs, out_specs=out_specs,
        compiler_params=_params("parallel"), name="ffn")(*args)
    return (outs[0], outs[1]) if has_next else (outs[0], None)


def _abin_body(layer, h_ref, w_ref, lbp_ref, gq_ref, gk_ref,
               qa_ref, lf_ref, va_ref, ga_ref, qb_ref, kb_ref, vb_ref, k_ref, v_ref):
    h = h_ref[...]
    tile, d = h.shape

    def seg(i):
        return _dot(h, w_ref[:, i * d:(i + 1) * d])

    lbp = lbp_ref[...]
    e = jnp.exp(lbp - jnp.max(lbp, axis=0, keepdims=True))
    lb = jnp.sum(e[:layer + 1], axis=0, keepdims=True) / jnp.sum(e, axis=0, keepdims=True)

    qa = seg(0)
    qa_ref[...] = (qa * _sigmoid(qa)).astype(BF16)
    f = lb + (1.0 - lb) * _sigmoid(seg(1))
    lf_ref[...] = jnp.log(f)
    va_ref[...] = seg(2).astype(BF16)
    ga_ref[...] = _sigmoid(seg(3)).astype(BF16)

    qb = seg(4)
    kb = seg(5)
    vb = seg(6)
    vb_ref[...] = vb.astype(BF16)
    for hh in range(N_HEADS):
        sl = slice(hh * HEAD_DIM, (hh + 1) * HEAD_DIM)
        qb_ref[:, sl] = (_rms_rows(qb[:, sl], gq_ref[...]) * HEAD_DIM ** -0.5).astype(BF16)
        kn = _rms_rows(kb[:, sl], gk_ref[...])
        kb_ref[:, sl] = kn.astype(BF16)
        k_ref[_head_rows(hh, tile), :] = kn
        v_ref[_head_rows(hh, tile), :] = vb[:, sl]


def _abin(h, w_in, lb_param, g_q, g_k, layer, tile):
    n, d = h.shape
    assert d == N_HEADS * HEAD_DIM
    feat = [BF16, F32, BF16, BF16, BF16, BF16, BF16]
    out_shape = [jax.ShapeDtypeStruct((n, d), t) for t in feat]
    out_specs = [_rows(tile, d) for _ in feat]
    for _ in range(2):
        out_shape.append(jax.ShapeDtypeStruct((n * N_HEADS, HEAD_DIM), F32))
        out_specs.append(_rows(tile * N_HEADS, HEAD_DIM))
    return pl.pallas_call(
        functools.partial(_abin_body, layer),
        out_shape=out_shape, grid=(n // tile,),
        in_specs=[_rows(tile, d), w_in[1], _resident(lb_param.shape), g_q[1], g_k[1]],
        out_specs=out_specs,
        compiler_params=_params("parallel"), name="ab_in_proj")(
            h, w_in[0], lb_param, g_q[0], g_k[0])


def _hgrn_scores_any_decay(q, k, lf2, chunk):
    row = lax.broadcasted_iota(jnp.int32, (chunk, chunk), 0)
    col = lax.broadcasted_iota(jnp.int32, (chunk, chunk), 1)
    t1 = lax.broadcasted_iota(jnp.int32, (chunk, 1), 0)
    scores = jnp.where(row == col, jnp.sum(q * k, axis=-1, keepdims=True), 0.0)
    half = chunk // 2
    while half >= 1:
        blk = 2 * half
        same = (row // blk) == (col // blk)
        mid = (row // blk) * blk + half
        right = row >= mid
        seg = same & ((right & (col >= mid) & (col <= row)) | ((col > row) & (col < mid)))
        d2 = _dot(jnp.where(seg, 1.0, 0.0).astype(BF16), lf2)
        decay = jnp.exp(d2[:, :HEAD_DIM] + d2[:, HEAD_DIM:])
        right1 = (t1 % blk) >= half
        qe = jnp.where(right1, q * decay, 0.0).astype(BF16)
        ke = jnp.where(right1, 0.0, k * decay).astype(BF16)
        scores = scores + jnp.where(same, _dot_nt(qe, ke), 0.0)
        half //= 2
    return scores


def _hgrn_body(chunk, n_chunks, heads, has_s0, factored, *refs):
    q_ref, lf_ref, v_ref, gt_ref = refs[:4]
    pos = 4
    s0_ref = refs[pos] if has_s0 else None
    pos += int(has_s0)
    go_ref, o_ref, so_ref = refs[pos:pos + 3]

    row = lax.broadcasted_iota(jnp.int32, (chunk, chunk), 0)
    col = lax.broadcasted_iota(jnp.int32, (chunk, chunk), 1)
    causal = col <= row
    lower = jnp.where(causal, 1.0, 0.0).astype(BF16)
    mid = chunk // 2
    head_cols = [slice(j * HEAD_DIM, (j + 1) * HEAD_DIM) for j in range(heads)]

    def step(i, states):
        rows = pl.ds(pl.multiple_of(i * chunk, chunk), chunk)
        lfs = [lf_ref[rows, cl] for cl in head_cols]
        lf2s = [jnp.concatenate(_split_bf16(lf), axis=1) for lf in lfs]
        c2s = [_dot(lower, lf2) for lf2 in lf2s]
        qs = [q_ref[rows, cl].astype(F32) for cl in head_cols]
        ks = [1.0 - jnp.exp(lf) for lf in lfs]
        cs = [c2[:, :HEAD_DIM] + c2[:, HEAD_DIM:] for c2 in c2s]
        c_lasts = [c[chunk - 1:chunk, :] for c in cs]
        vs = [v_ref[rows, cl] for cl in head_cols]
        if factored:
            c_mids = [c[mid - 1:mid, :] for c in cs]
            rel = [c - cm for c, cm in zip(cs, c_mids)]
            qes = [(q * jnp.exp(a)).astype(BF16) for q, a in zip(qs, rel)]
            kes = [(k * jnp.exp(-a)).astype(BF16) for k, a in zip(ks, rel)]
            rhs = [jnp.concatenate([ke, (st * jnp.exp(cm)).astype(BF16)], axis=0)
                   for ke, st, cm in zip(kes, states, c_mids)]
            both = [_dot_nt(qe, r) for qe, r in zip(qes, rhs)]
            grown = [_dot_tn(v, ke) * jnp.exp(cl - cm) for v, ke, cl, cm in zip(vs, kes, c_lasts, c_mids)]
            scores = [jnp.where(causal, bt[:, :chunk], 0.0) for bt in both]
            inter = [bt[:, chunk:] for bt in both]
        else:
            inter = [_dot_nt((q * jnp.exp(c)).astype(BF16), st.astype(BF16)) for q, c, st in zip(qs, cs, states)]
            scores = [_hgrn_scores_any_decay(q, k, lf2, chunk) for q, k, lf2 in zip(qs, ks, lf2s)]
            kds = [(k * jnp.exp(cl - c)).astype(BF16) for k, cl, c in zip(ks, c_lasts, cs)]
            grown = [_dot_tn(v, kd) for v, kd in zip(vs, kds)]
        intra = [_dot(s.astype(BF16), v) for s, v in zip(scores, vs)]
        for j, cl in enumerate(head_cols):
            o = _rms_rows(inter[j] + intra[j], go_ref[j]) * gt_ref[rows, cl].astype(F32)
            o_ref[rows, cl] = o.astype(o_ref.dtype)
        return tuple(st * jnp.exp(cl) + g for st, cl, g in zip(states, c_lasts, grown))

    if has_s0:
        states = tuple(s0_ref[j].T for j in range(heads))
    else:
        states = tuple(jnp.zeros((HEAD_DIM, HEAD_DIM), F32) for _ in range(heads))
    states = lax.fori_loop(0, n_chunks, step, states) if n_chunks > 1 else step(0, states)
    for j in range(heads):
        so_ref[j] = states[j].T


def _hgrn_call(qa, lf, va, ga, s0, g_out, batch, seq, chunk, heads, factored):
    n, d = qa.shape
    has_s0 = s0 is not None
    width = heads * HEAD_DIM
    tok = pl.BlockSpec((seq, width), lambda b, h: (b, h))
    state = pl.BlockSpec((None, heads, HEAD_DIM, HEAD_DIM), lambda b, h: (b, h, 0, 0))
    in_specs = [tok, tok, tok, tok]
    args = [qa, lf, va, ga]
    if has_s0:
        in_specs.append(state)
        args.append(s0)
    in_specs.append(pl.BlockSpec((heads, 1, HEAD_DIM), lambda b, h: (h, 0, 0)))
    args.append(g_out.reshape(N_HEADS, 1, HEAD_DIM))
    return pl.pallas_call(
        functools.partial(_hgrn_body, chunk, seq // chunk, heads, has_s0, factored),
        out_shape=[jax.ShapeDtypeStruct((n, d), BF16),
                   jax.ShapeDtypeStruct((batch, N_HEADS, HEAD_DIM, HEAD_DIM), F32)],
        grid=(batch, N_HEADS // heads), in_specs=in_specs, out_specs=[tok, state],
        compiler_params=_params("parallel", "parallel"), name="hgrn2")(*args)


def _hgrn(qa, lf, va, ga, s0, g_out, lb_param, layer, batch, seq):
    chunk = min(seq, HGRN_CHUNK)
    heads = N_HEADS if seq < HGRN_CHUNK else HGRN_HEADS_PER_STEP
    call = functools.partial(_hgrn_call, qa, lf, va, ga, s0, g_out, batch, seq, chunk, heads)
    lb = jnp.cumsum(jax.nn.softmax(lb_param.astype(F32), axis=0), axis=0)[layer]
    reach = (chunk // 2) * jnp.max(-jnp.log(lb))
    return lax.cond(reach <= HGRN_SAFE_EXPONENT, lambda: call(True), lambda: call(False))


def _sb_logits_many(items):
    zs = [_dot_nt(q, kb) for q, kb, _, _ in items]
    sps = [jnp.maximum(z, 0.0) + jnp.log(1.0 + jnp.exp(-jnp.abs(z))) for z in zs]
    stays = [-sp if it[3] is None else jnp.where(it[3], -sp, 0.0) for sp, it in zip(sps, items)]
    splits = [_split_bf16(s) for s in stays]
    suffix = [_dot(hi, it[2]) + _dot(lo, it[2]) for (hi, lo), it in zip(splits, items)]
    return [(z - sp + sf, jnp.sum(s, axis=-1, keepdims=True)) for z, sp, sf, s in zip(zs, sps, suffix, stays)]


def _sb_accumulate(logits, stay, vb, mask, acc, run):
    w = jnp.exp(logits + run)
    if mask is not None:
        w = jnp.where(mask, w, 0.0)
    return acc + _dot(w.astype(BF16), vb), run + stay


def _strict_upper(n):
    j = lax.broadcasted_iota(jnp.int32, (n, n), 0)
    k = lax.broadcasted_iota(jnp.int32, (n, n), 1)
    return jnp.where(j > k, 1.0, 0.0).astype(BF16)


def _sb_body(blk, n_q, heads, past_blk, n_past, *refs):
    q_ref, k_ref, v_ref = refs[:3]
    if n_past:
        kp_new_ref, vp_new_ref, kp_hbm, vp_hbm, o_ref, kbuf, vbuf, sem = refs[3:]
    else:
        o_ref = refs[3]

    t = lax.broadcasted_iota(jnp.int32, (blk, blk), 0)
    s = lax.broadcasted_iota(jnp.int32, (blk, blk), 1)
    diag_mask = s < t
    upper = _strict_upper(blk)
    upper_past = _strict_upper(past_blk) if n_past else None
    head_cols = [slice(j * HEAD_DIM, (j + 1) * HEAD_DIM) for j in range(heads)]

    def still_live(runs):
        return functools.reduce(jnp.maximum, [jnp.max(r) for r in runs]) > SB_EXIT

    for qi in range(n_q):
        rows = pl.ds(qi * blk, blk)
        qs = [q_ref[rows, c] for c in head_cols]
        n_older = qi + n_past
        items = [(qs[j], k_ref[rows, c], upper, diag_mask) for j, c in enumerate(head_cols)]
        values = [(v_ref[rows, c], diag_mask) for c in head_cols]
        if qi > 0:
            prev = pl.ds((qi - 1) * blk, blk)
            items += [(qs[j], k_ref[prev, c], upper, None) for j, c in enumerate(head_cols)]
            values += [(v_ref[prev, c], None) for c in head_cols]
        elif n_past:
            items += [(qs[j], kp_new_ref[_head_rows(j, past_blk), :].astype(BF16), upper_past, None)
                      for j in range(heads)]
            values += [(vp_new_ref[_head_rows(j, past_blk), :].astype(BF16), None) for j in range(heads)]
        logits = _sb_logits_many(items)
        accs = [jnp.zeros((blk, HEAD_DIM), F32) for _ in range(heads)]
        runs = [jnp.zeros((blk, 1), F32) for _ in range(heads)]
        for n, ((lg, stay), (vb, mask)) in enumerate(zip(logits, values)):
            j = n % heads
            accs[j], runs[j] = _sb_accumulate(lg, stay, vb, mask, accs[j], runs[j])

        if n_older > 1:
            def cond(carry):
                return (carry[0] < n_older) & carry[1]

            def body(carry, qi=qi, qs=qs):
                idx = carry[0]
                accs, runs = list(carry[2]), list(carry[3])
                if n_past:
                    src = pl.ds(pl.multiple_of((n_past - 1 - idx) * past_blk * N_HEADS, past_blk * N_HEADS),
                                past_blk * N_HEADS)
                    b = pl.program_id(0)
                    copies = [pltpu.make_async_copy(kp_hbm.at[b, src, :], kbuf, sem.at[0]),
                              pltpu.make_async_copy(vp_hbm.at[b, src, :], vbuf, sem.at[1])]
                    for cp in copies:
                        cp.start()
                    for cp in copies:
                        cp.wait()
                if n_past:
                    items = [(qs[j], kbuf[_head_rows(j, past_blk), :].astype(BF16), upper_past, None)
                             for j in range(heads)]
                    vbs = [vbuf[_head_rows(j, past_blk), :].astype(BF16) for j in range(heads)]
                else:
                    kr = pl.ds(pl.multiple_of((qi - 1 - idx) * blk, blk), blk)
                    items = [(qs[j], k_ref[kr, c], upper, None) for j, c in enumerate(head_cols)]
                    vbs = [v_ref[kr, c] for c in head_cols]
                for j, (lg, stay) in enumerate(_sb_logits_many(items)):
                    accs[j], runs[j] = _sb_accumulate(lg, stay, vbs[j], None, accs[j], runs[j])
                return idx + 1, still_live(runs), tuple(accs), tuple(runs)

            _, _, accs, _ = lax.while_loop(cond, body, (jnp.int32(1), still_live(runs), tuple(accs), tuple(runs)))
        for j, c in enumerate(head_cols):
            o_ref[rows, c] = accs[j].astype(o_ref.dtype)


def _stick_breaking(qn, kb, vb, k_past, v_past, batch, seq):
    n, d = qn.shape
    blk = min(seq, SB_BLOCK)
    n_q = seq // blk
    heads = N_HEADS if seq < SB_BLOCK else SB_HEADS_PER_STEP
    tok = pl.BlockSpec((seq, heads * HEAD_DIM), lambda b, h: (b, h))
    in_specs = [tok, tok, tok]
    args = [qn, kb, vb]
    scratch = []
    past_blk = n_past = 0
    if k_past is not None:
        assert n_q == 1 and heads == N_HEADS, "cached keys are only supported for a single query block"
        past_len = k_past.shape[1]
        past_blk = min(past_len, SB_BLOCK)
        n_past = past_len // past_blk
        slab = past_blk * N_HEADS
        newest = pl.BlockSpec((None, slab, HEAD_DIM), lambda b, h: (b, n_past - 1, 0))
        hbm = pl.BlockSpec(memory_space=pl.ANY)
        in_specs += [newest, newest, hbm, hbm]
        kp = k_past.reshape(batch, past_len * N_HEADS, HEAD_DIM)
        vp = v_past.reshape(batch, past_len * N_HEADS, HEAD_DIM)
        args += [kp, vp, kp, vp]
        scratch = [pltpu.VMEM((slab, HEAD_DIM), F32), pltpu.VMEM((slab, HEAD_DIM), F32),
                   pltpu.SemaphoreType.DMA((2,))]
    return pl.pallas_call(
        functools.partial(_sb_body, blk, n_q, heads, past_blk, n_past),
        out_shape=jax.ShapeDtypeStruct((n, d), BF16),
        grid=(batch, N_HEADS // heads), in_specs=in_specs, out_specs=tok,
        scratch_shapes=scratch,
        compiler_params=_params("parallel", "parallel"), name="stick_breaking")(*args)


def _gelu_tanh(x):
    return 0.5 * x * (1.0 + jnp.tanh(0.7978845608028654 * (x + 0.044715 * (x * x * x))))


def _gmlp_body(length, n_sub, emit_v, h_ref, w_ref, g_ref, b_ref, ws_ref, bs_ref, y_ref, *v_out):
    h = h_ref[...]
    d_c = w_ref.shape[1] // 2
    dg = d_c // GMLP_GROUPS
    i = lax.broadcasted_iota(jnp.int32, (length, length), 0)
    j = lax.broadcasted_iota(jnp.int32, (length, length), 1)
    mask = (j // CHUNK) <= (i // CHUNK)
    bias = bs_ref[...]

    vs = [_gelu_tanh(_dot(h, w_ref[:, d_c + g * dg:d_c + (g + 1) * dg])) for g in range(GMLP_GROUPS)]
    total = functools.reduce(jnp.add, [jnp.sum(v, axis=-1, keepdims=True) for v in vs])
    total_sq = functools.reduce(jnp.add, [jnp.sum(v * v, axis=-1, keepdims=True) for v in vs])
    mean = total / d_c
    inv = lax.rsqrt(total_sq / d_c - mean * mean + EPS)
    for g in range(GMLP_GROUPS):
        cols = slice(g * dg, (g + 1) * dg)
        vn = (vs[g] - mean) * inv * g_ref[:, cols] + b_ref[:, cols]
        if emit_v:
            v_out[0][:, cols] = vn
        vn = vn.astype(BF16)
        u = _gelu_tanh(_dot(h, w_ref[:, cols]))
        w = jnp.where(mask, ws_ref[g], 0.0).astype(BF16)
        for c in range(n_sub):
            rows = slice(c * length, (c + 1) * length)
            mixed = _dot(w, vn[rows]) + bias[:, g:g + 1]
            y_ref[rows, cols] = (u[rows] * mixed).astype(BF16)


def _gmlp(h, w_in, ln_g, ln_b, w_s, b_s, length, tile, emit_v):
    n, d = h.shape
    d_c = w_in[0].shape[-1] // 2
    bs = b_s.T
    out_shape = [jax.ShapeDtypeStruct((n, d_c), BF16)]
    out_specs = [_rows(tile, d_c)]
    if emit_v:
        out_shape.append(jax.ShapeDtypeStruct((n, d_c), F32))
        out_specs.append(_rows(tile, d_c))
    outs = pl.pallas_call(
        functools.partial(_gmlp_body, length, tile // length, emit_v),
        out_shape=out_shape, grid=(n // tile,),
        in_specs=[_rows(tile, d), w_in[1], ln_g[1], ln_b[1], _resident(w_s.shape), _resident(bs.shape)],
        out_specs=out_specs,
        compiler_params=_params("parallel"), name="gmlp")(h, w_in[0], ln_g[0], ln_b[0], w_s, bs)
    return (outs[0], outs[1]) if emit_v else (outs[0], None)


def _trunk(x3, p, hgrn_state, sb_k, sb_v, want_cv):
    batch, seq, d = x3.shape
    n = batch * seq
    tile = min(TOKEN_TILE, n)
    depth = p['ffn1_norm'].shape[0]
    x = x3.reshape(n, d)
    new_k, new_v, new_s, new_cv = [], [], [], []
    pre = []

    def row(name, l):
        v = p[name]
        return _layer(v.reshape(v.shape[0], 1, -1), l)

    for l in range(depth):
        x, h = _ffn(x, pre, row('ffn1_norm', l), _layer(p['ffn1_w_gate'], l), _layer(p['ffn1_w_up'], l),
                    _layer(p['ffn1_w_down'], l), row('mix_norm', l), tile)
        if l % 2 == 0:
            e = l // 2
            qa, lf, va, ga, qb, kb, vb, k, v = _abin(h, _layer(p['ab_w_in'], e), p['ab_lb'], row('ab_g_q', e),
                                                     row('ab_g_k', e), e, tile)
            mix_a, s_new = _hgrn(qa, lf, va, ga, None if hgrn_state is None else hgrn_state[e],
                                 p['ab_g_out'][e], p['ab_lb'], e, batch, seq)
            mix_b = _stick_breaking(qb, kb, vb, None if sb_k is None else sb_k[e],
                                    None if sb_v is None else sb_v[e], batch, seq)
            pre = [(mix_a, _layer(p['ab_w_out'], e, 0, 2)), (mix_b, _layer(p['ab_w_out'], e, 1, 2))]
            new_k.append(k.reshape(batch, seq, N_HEADS, HEAD_DIM))
            new_v.append(v.reshape(batch, seq, N_HEADS, HEAD_DIM))
            new_s.append(s_new)
        else:
            o = l // 2
            length = min(seq, GMLP_LEN)
            y, cv = _gmlp(h, _layer(p['c_w_in'], o), row('c_ln_g', o), row('c_ln_b', o),
                          p['c_w_s'][o, :, :length, :length], p['c_b_s'][o, :, :length], length, tile, want_cv)
            pre = [(y, _layer(p['c_w_out'], o))]
            if want_cv:
                new_cv.append(cv.reshape(batch, seq, -1))
        x, _ = _ffn(x, pre, row('ffn2_norm', l), _layer(p['ffn2_w_gate'], l), _layer(p['ffn2_w_up'], l),
                    _layer(p['ffn2_w_down'], l), None, tile)
        pre = []
    return x.reshape(batch, seq, d), new_k, new_v, new_s, new_cv


_MATMUL_WEIGHTS = ('ffn1_w_gate', 'ffn1_w_up', 'ffn1_w_down', 'ffn2_w_gate', 'ffn2_w_up',
                   'ffn2_w_down', 'ab_w_in', 'ab_w_out', 'c_w_in', 'c_w_out')


def kernel(x_prompt, x_sample, cache_sb_k, cache_sb_v, state_hgrn, ffn1_norm, ffn1_w_gate, ffn1_w_up, ffn1_w_down, mix_norm, ffn2_norm, ffn2_w_gate, ffn2_w_up, ffn2_w_down, ab_w_in, ab_lb, ab_g_out, ab_g_q, ab_g_k, ab_w_out, c_w_in, c_ln_g, c_ln_b, c_w_s, c_b_s, c_w_out):
    p = {'ffn1_norm': ffn1_norm, 'ffn1_w_gate': ffn1_w_gate, 'ffn1_w_up': ffn1_w_up, 'ffn1_w_down': ffn1_w_down,
         'mix_norm': mix_norm, 'ffn2_norm': ffn2_norm, 'ffn2_w_gate': ffn2_w_gate, 'ffn2_w_up': ffn2_w_up,
         'ffn2_w_down': ffn2_w_down, 'ab_w_in': ab_w_in, 'ab_lb': ab_lb, 'ab_g_out': ab_g_out, 'ab_g_q': ab_g_q,
         'ab_g_k': ab_g_k, 'ab_w_out': ab_w_out, 'c_w_in': c_w_in, 'c_ln_g': c_ln_g, 'c_ln_b': c_ln_b,
         'c_w_s': c_w_s, 'c_b_s': c_b_s, 'c_w_out': c_w_out}
    for name in _MATMUL_WEIGHTS:
        p[name] = p[name].astype(BF16)
    y_prompt, pk, pv, ps, _ = _trunk(x_prompt, p, None, None, None, False)
    y_sample, sk, sv, ss, scv = _trunk(x_sample, p, state_hgrn, cache_sb_k, cache_sb_v, True)
    return (y_prompt, y_sample, jnp.stack(pk), jnp.stack(pv), jnp.stack(ps),
            jnp.stack(sk), jnp.stack(sv), jnp.stack(ss), jnp.stack(scv))
```

```python
import functools

import jax
import jax.numpy as jnp
from jax import lax
from jax.experimental import pallas as pl
from jax.experimental.pallas import tpu as pltpu

F32 = jnp.float32
BF16 = jnp.bfloat16

EPS = 1e-6
FFN_SCALE = 0.5
HEAD_DIM = 128
N_HEADS = 8
CHUNK = 64
GMLP_LEN = 128
GMLP_GROUPS = 8
V7X_VMEM_LIMIT_BYTES = 56 * 1024 * 1024
TOKEN_TILE = 512
HGRN_CHUNK = 128
HGRN_HEADS_PER_STEP = 4
HGRN_SAFE_EXPONENT = 60.0
SB_BLOCK = 256
SB_HEADS_PER_STEP = 2
SB_EXIT = -105.0


def _params(*sem):
    return pltpu.CompilerParams(dimension_semantics=sem, vmem_limit_bytes=V7X_VMEM_LIMIT_BYTES)


def _resident(shape):
    zeros = (0,) * len(shape)
    return pl.BlockSpec(shape, lambda *_: zeros, pipeline_mode=pl.Buffered(1))


def _layer(stacked, layer, row_block=0, n_row_blocks=1):
    rows = stacked.shape[1] // n_row_blocks
    block = (None, rows) + stacked.shape[2:]
    index = (layer, row_block) + (0,) * (stacked.ndim - 2)
    return stacked, pl.BlockSpec(block, lambda *_: index, pipeline_mode=pl.Buffered(1))


def _rows(tile, width):
    return pl.BlockSpec((tile, width), lambda i: (i, 0))


def _sigmoid(x):
    return 1.0 / (1.0 + jnp.exp(-x))


def _rms_rows(x, g):
    ms = jnp.mean(x * x, axis=-1, keepdims=True)
    return x * lax.rsqrt(ms + EPS) * g


def _dot(a, b):
    return jnp.dot(a, b, preferred_element_type=F32)


def _dot_nt(a, b):
    return lax.dot_general(a, b, (((1,), (1,)), ((), ())), preferred_element_type=F32)


def _dot_tn(a, b):
    return lax.dot_general(a, b, (((0,), (0,)), ((), ())), preferred_element_type=F32)


def _split_bf16(x):
    hi = x.astype(BF16)
    lo = (x - hi.astype(F32)).astype(BF16)
    return hi, lo


def _head_rows(head, n):
    return pl.ds(head, n, stride=N_HEADS)


def _ffn_body(n_pre, has_next, *refs):
    x_ref = refs[0]
    pre = refs[1:1 + 2 * n_pre]
    g_ref, wg_ref, wu_ref, wd_ref = refs[1 + 2 * n_pre:5 + 2 * n_pre]
    pos = 5 + 2 * n_pre
    g2_ref = refs[pos] if has_next else None
    pos += int(has_next)
    xo_ref = refs[pos]
    ho_ref = refs[pos + 1] if has_next else None

    x = x_ref[...]
    for i in range(n_pre):
        x = x + _dot(pre[2 * i][...], pre[2 * i + 1][...])
    h = _rms_rows(x, g_ref[...]).astype(BF16)
    a = _dot(h, wg_ref[...])
    b = _dot(h, wu_ref[...])
    act = (a * _sigmoid(a) * b).astype(BF16)
    x = x + FFN_SCALE * _dot(act, wd_ref[...])
    xo_ref[...] = x
    if has_next:
        ho_ref[...] = _rms_rows(x, g2_ref[...]).astype(BF16)


def _ffn(x, pre, g, wg, wu, wd, g_next, tile):
    n, d = x.shape
    has_next = g_next is not None
    in_specs = [_rows(tile, d)]
    args = [x]
    for m, (w, w_spec) in pre:
        in_specs += [_rows(tile, m.shape[1]), w_spec]
        args += [m, w]
    for arr, spec in (g, wg, wu, wd):
        in_specs.append(spec)
        args.append(arr)
    out_shape = [jax.ShapeDtypeStruct((n, d), F32)]
    out_specs = [_rows(tile, d)]
    if has_next:
        in_specs.append(g_next[1])
        args.append(g_next[0])
        out_shape.append(jax.ShapeDtypeStruct((n, d), BF16))
        out_specs.append(_rows(tile, d))
    outs = pl.pallas_call(
        functools.partial(_ffn_body, len(pre), has_next),
        out_shape=out_shape, grid=(n // tile,), in_specs=in_specs, out_specs=out_specs,
        compiler_params=_params("parallel"), name="ffn")(*args)
    return (outs[0], outs[1]) if has_next else (outs[0], None)


def _abin_body(layer, h_ref, w_ref, lbp_ref, gq_ref, gk_ref,
               qa_ref, lf_ref, va_ref, ga_ref, qb_ref, kb_ref, vb_ref, k_ref, v_ref):
    h = h_ref[...]
    tile, d = h.shape

    def seg(i):
        return _dot(h, w_ref[:, i * d:(i + 1) * d])

    lbp = lbp_ref[...]
    e = jnp.exp(lbp - jnp.max(lbp, axis=0, keepdims=True))
    lb = jnp.sum(e[:layer + 1], axis=0, keepdims=True) / jnp.sum(e, axis=0, keepdims=True)

    qa = seg(0)
    qa_ref[...] = (qa * _sigmoid(qa)).astype(BF16)
    f = lb + (1.0 - lb) * _sigmoid(seg(1))
    lf_ref[...] = jnp.log(f)
    va_ref[...] = seg(2).astype(BF16)
    ga_ref[...] = _sigmoid(seg(3)).astype(BF16)

    qb = seg(4)
    kb = seg(5)
    vb = seg(6)
    vb_ref[...] = vb.astype(BF16)
    for hh in range(N_HEADS):
        sl = slice(hh * HEAD_DIM, (hh + 1) * HEAD_DIM)
        qb_ref[:, sl] = (_rms_rows(qb[:, sl], gq_ref[...]) * HEAD_DIM ** -0.5).astype(BF16)
        kn = _rms_rows(kb[:, sl], gk_ref[...])
        kb_ref[:, sl] = kn.astype(BF16)
        k_ref[_head_rows(hh, tile), :] = kn
        v_ref[_head_rows(hh, tile), :] = vb[:, sl]


def _abin(h, w_in, lb_param, g_q, g_k, layer, tile):
    n, d = h.shape
    assert d == N_HEADS * HEAD_DIM
    feat = [BF16, F32, BF16, BF16, BF16, BF16, BF16]
    out_shape = [jax.ShapeDtypeStruct((n, d), t) for t in feat]
    out_specs = [_rows(tile, d) for _ in feat]
    for _ in range(2):
        out_shape.append(jax.ShapeDtypeStruct((n * N_HEADS, HEAD_DIM), F32))
        out_specs.append(_rows(tile * N_HEADS, HEAD_DIM))
    return pl.pallas_call(
        functools.partial(_abin_body, layer),
        out_shape=out_shape, grid=(n // tile,),
        in_specs=[_rows(tile, d), w_in[1], _resident(lb_param.shape), g_q[1], g_k[1]],
        out_specs=out_specs,
        compiler_params=_params("parallel"), name="ab_in_proj")(
            h, w_in[0], lb_param, g_q[0], g_k[0])


def _hgrn_scores_any_decay(q, k, lf2, chunk):
    row = lax.broadcasted_iota(jnp.int32, (chunk, chunk), 0)
    col = lax.broadcasted_iota(jnp.int32, (chunk, chunk), 1)
    t1 = lax.broadcasted_iota(jnp.int32, (chunk, 1), 0)
    scores = jnp.where(row == col, jnp.sum(q * k, axis=-1, keepdims=True), 0.0)
    half = chunk // 2
    while half >= 1:
        blk = 2 * half
        same = (row // blk) == (col // blk)
        mid = (row // blk) * blk + half
        right = row >= mid
        seg = same & ((right & (col >= mid) & (col <= row)) | ((col > row) & (col < mid)))
        d2 = _dot(jnp.where(seg, 1.0, 0.0).astype(BF16), lf2)
        decay = jnp.exp(d2[:, :HEAD_DIM] + d2[:, HEAD_DIM:])
        right1 = (t1 % blk) >= half
        qe = jnp.where(right1, q * decay, 0.0).astype(BF16)
        ke = jnp.where(right1, 0.0, k * decay).astype(BF16)
        scores = scores + jnp.where(same, _dot_nt(qe, ke), 0.0)
        half //= 2
    return scores


def _hgrn_body(chunk, n_chunks, heads, has_s0, factored, *refs):
    q_ref, lf_ref, v_ref, gt_ref = refs[:4]
    pos = 4
    s0_ref = refs[pos] if has_s0 else None
    pos += int(has_s0)
    go_ref, o_ref, so_ref = refs[pos:pos + 3]

    row = lax.broadcasted_iota(jnp.int32, (chunk, chunk), 0)
    col = lax.broadcasted_iota(jnp.int32, (chunk, chunk), 1)
    causal = col <= row
    lower = jnp.where(causal, 1.0, 0.0).astype(BF16)
    mid = chunk // 2
    head_cols = [slice(j * HEAD_DIM, (j + 1) * HEAD_DIM) for j in range(heads)]

    def step(i, states):
        rows = pl.ds(pl.multiple_of(i * chunk, chunk), chunk)
        lfs = [lf_ref[rows, cl] for cl in head_cols]
        lf2s = [jnp.concatenate(_split_bf16(lf), axis=1) for lf in lfs]
        c2s = [_dot(lower, lf2) for lf2 in lf2s]
        qs = [q_ref[rows, cl].astype(F32) for cl in head_cols]
        ks = [1.0 - jnp.exp(lf) for lf in lfs]
        cs = [c2[:, :HEAD_DIM] + c2[:, HEAD_DIM:] for c2 in c2s]
        c_lasts = [c[chunk - 1:chunk, :] for c in cs]
        vs = [v_ref[rows, cl] for cl in head_cols]
        if factored:
            c_mids = [c[mid - 1:mid, :] for c in cs]
            rel = [c - cm for c, cm in zip(cs, c_mids)]
            qes = [(q * jnp.exp(a)).astype(BF16) for q, a in zip(qs, rel)]
            kes = [(k * jnp.exp(-a)).astype(BF16) for k, a in zip(ks, rel)]
            rhs = [jnp.concatenate([ke, (st * jnp.exp(cm)).astype(BF16)], axis=0)
                   for ke, st, cm in zip(kes, states, c_mids)]
            both = [_dot_nt(qe, r) for qe, r in zip(qes, rhs)]
            grown = [_dot_tn(v, ke) * jnp.exp(cl - cm) for v, ke, cl, cm in zip(vs, kes, c_lasts, c_mids)]
            scores = [jnp.where(causal, bt[:, :chunk], 0.0) for bt in both]
            inter = [bt[:, chunk:] for bt in both]
        else:
            inter = [_dot_nt((q * jnp.exp(c)).astype(BF16), st.astype(BF16)) for q, c, st in zip(qs, cs, states)]
            scores = [_hgrn_scores_any_decay(q, k, lf2, chunk) for q, k, lf2 in zip(qs, ks, lf2s)]
            kds = [(k * jnp.exp(cl - c)).astype(BF16) for k, cl, c in zip(ks, c_lasts, cs)]
            grown = [_dot_tn(v, kd) for v, kd in zip(vs, kds)]
        intra = [_dot(s.astype(BF16), v) for s, v in zip(scores, vs)]
        for j, cl in enumerate(head_cols):
            o = _rms_rows(inter[j] + intra[j], go_ref[j]) * gt_ref[rows, cl].astype(F32)
            o_ref[rows, cl] = o.astype(o_ref.dtype)
        return tuple(st * jnp.exp(cl) + g for st, cl, g in zip(states, c_lasts, grown))

    if has_s0:
        states = tuple(s0_ref[j].T for j in range(heads))
    else:
        states = tuple(jnp.zeros((HEAD_DIM, HEAD_DIM), F32) for _ in range(heads))
    states = lax.fori_loop(0, n_chunks, step, states, unroll=4) if n_chunks > 1 else step(0, states)
    for j in range(heads):
        so_ref[j] = states[j].T


def _hgrn_call(qa, lf, va, ga, s0, g_out, batch, seq, chunk, heads, factored):
    n, d = qa.shape
    has_s0 = s0 is not None
    width = heads * HEAD_DIM
    tok = pl.BlockSpec((seq, width), lambda b, h: (b, h))
    state = pl.BlockSpec((None, heads, HEAD_DIM, HEAD_DIM), lambda b, h: (b, h, 0, 0))
    in_specs = [tok, tok, tok, tok]
    args = [qa, lf, va, ga]
    if has_s0:
        in_specs.append(state)
        args.append(s0)
    in_specs.append(pl.BlockSpec((heads, 1, HEAD_DIM), lambda b, h: (h, 0, 0)))
    args.append(g_out.reshape(N_HEADS, 1, HEAD_DIM))
    return pl.pallas_call(
        functools.partial(_hgrn_body, chunk, seq // chunk, heads, has_s0, factored),
        out_shape=[jax.ShapeDtypeStruct((n, d), BF16),
                   jax.ShapeDtypeStruct((batch, N_HEADS, HEAD_DIM, HEAD_DIM), F32)],
        grid=(batch, N_HEADS // heads), in_specs=in_specs, out_specs=[tok, state],
        compiler_params=_params("parallel", "parallel"), name="hgrn2")(*args)


def _hgrn(qa, lf, va, ga, s0, g_out, lb_param, layer, batch, seq):
    chunk = min(seq, HGRN_CHUNK)
    heads = N_HEADS if seq < HGRN_CHUNK else HGRN_HEADS_PER_STEP
    call = functools.partial(_hgrn_call, qa, lf, va, ga, s0, g_out, batch, seq, chunk, heads)
    lb = jnp.cumsum(jax.nn.softmax(lb_param.astype(F32), axis=0), axis=0)[layer]
    reach = (chunk // 2) * jnp.max(-jnp.log(lb))
    return lax.cond(reach <= HGRN_SAFE_EXPONENT, lambda: call(True), lambda: call(False))


def _sb_logits_many(items):
    zs = [_dot_nt(q, kb) for q, kb, _, _ in items]
    sps = [jnp.maximum(z, 0.0) + jnp.log(1.0 + jnp.exp(-jnp.abs(z))) for z in zs]
    costs = [sp if it[3] is None else jnp.where(it[3], sp, 0.0) for sp, it in zip(sps, items)]
    suffix = [_dot(c.astype(BF16), it[2]) for c, it in zip(costs, items)]
    return [(z - sp - sf, jnp.sum(c, axis=-1, keepdims=True)) for z, sp, sf, c in zip(zs, sps, suffix, costs)]


def _sb_accumulate(logits, cost, vb, mask, acc, spent):
    w = jnp.exp(logits - spent)
    if mask is not None:
        w = jnp.where(mask, w, 0.0)
    return acc + _dot(w.astype(BF16), vb), spent + cost


def _strict_upper(n):
    j = lax.broadcasted_iota(jnp.int32, (n, n), 0)
    k = lax.broadcasted_iota(jnp.int32, (n, n), 1)
    return jnp.where(j > k, 1.0, 0.0).astype(BF16)


def _sb_body(blk, n_q, heads, past_blk, n_past, *refs):
    q_ref, k_ref, v_ref = refs[:3]
    if n_past:
        kp_new_ref, vp_new_ref, kp_hbm, vp_hbm, o_ref, kbuf, vbuf, sem = refs[3:]
    else:
        o_ref = refs[3]

    t = lax.broadcasted_iota(jnp.int32, (blk, blk), 0)
    s = lax.broadcasted_iota(jnp.int32, (blk, blk), 1)
    diag_mask = s < t
    upper = _strict_upper(blk)
    upper_past = _strict_upper(past_blk) if n_past else None
    head_cols = [slice(j * HEAD_DIM, (j + 1) * HEAD_DIM) for j in range(heads)]

    def still_live(spent):
        return functools.reduce(jnp.minimum, [jnp.min(r) for r in spent]) < -SB_EXIT

    for qi in range(n_q):
        rows = pl.ds(qi * blk, blk)
        qs = [q_ref[rows, c] for c in head_cols]
        n_older = qi + n_past
        items = [(qs[j], k_ref[rows, c], upper, diag_mask) for j, c in enumerate(head_cols)]
        values = [(v_ref[rows, c], diag_mask) for c in head_cols]
        if qi > 0:
            prev = pl.ds((qi - 1) * blk, blk)
            items += [(qs[j], k_ref[prev, c], upper, None) for j, c in enumerate(head_cols)]
            values += [(v_ref[prev, c], None) for c in head_cols]
        elif n_past:
            items += [(qs[j], kp_new_ref[_head_rows(j, past_blk), :].astype(BF16), upper_past, None)
                      for j in range(heads)]
            values += [(vp_new_ref[_head_rows(j, past_blk), :].astype(BF16), None) for j in range(heads)]
        logits = _sb_logits_many(items)
        accs = [jnp.zeros((blk, HEAD_DIM), F32) for _ in range(heads)]
        runs = [jnp.zeros((blk, 1), F32) for _ in range(heads)]
        for n, ((lg, stay), (vb, mask)) in enumerate(zip(logits, values)):
            j = n % heads
            accs[j], runs[j] = _sb_accumulate(lg, stay, vb, mask, accs[j], runs[j])

        if n_older > 1:
            def cond(carry):
                return (carry[0] < n_older) & carry[1]

            def body(carry, qi=qi, qs=qs):
                idx = carry[0]
                accs, runs = list(carry[2]), list(carry[3])
                if n_past:
                    src = pl.ds(pl.multiple_of((n_past - 1 - idx) * past_blk * N_HEADS, past_blk * N_HEADS),
                                past_blk * N_HEADS)
                    b = pl.program_id(0)
                    copies = [pltpu.make_async_copy(kp_hbm.at[b, src, :], kbuf, sem.at[0]),
                              pltpu.make_async_copy(vp_hbm.at[b, src, :], vbuf, sem.at[1])]
                    for cp in copies:
                        cp.start()
                    for cp in copies:
                        cp.wait()
                if n_past:
                    items = [(qs[j], kbuf[_head_rows(j, past_blk), :].astype(BF16), upper_past, None)
                             for j in range(heads)]
                    vbs = [vbuf[_head_rows(j, past_blk), :].astype(BF16) for j in range(heads)]
                else:
                    kr = pl.ds(pl.multiple_of((qi - 1 - idx) * blk, blk), blk)
                    items = [(qs[j], k_ref[kr, c], upper, None) for j, c in enumerate(head_cols)]
                    vbs = [v_ref[kr, c] for c in head_cols]
                for j, (lg, stay) in enumerate(_sb_logits_many(items)):
                    accs[j], runs[j] = _sb_accumulate(lg, stay, vbs[j], None, accs[j], runs[j])
                return idx + 1, still_live(runs), tuple(accs), tuple(runs)

            _, _, accs, _ = lax.while_loop(cond, body, (jnp.int32(1), still_live(runs), tuple(accs), tuple(runs)))
        for j, c in enumerate(head_cols):
            o_ref[rows, c] = accs[j].astype(o_ref.dtype)


def _stick_breaking(qn, kb, vb, k_past, v_past, batch, seq):
    n, d = qn.shape
    blk = min(seq, SB_BLOCK)
    n_q = seq // blk
    heads = N_HEADS if seq < SB_BLOCK else SB_HEADS_PER_STEP
    tok = pl.BlockSpec((seq, heads * HEAD_DIM), lambda b, h: (b, h))
    in_specs = [tok, tok, tok]
    args = [qn, kb, vb]
    scratch = []
    past_blk = n_past = 0
    if k_past is not None:
        assert n_q == 1 and heads == N_HEADS, "cached keys are only supported for a single query block"
        past_len = k_past.shape[1]
        past_blk = min(past_len, SB_BLOCK)
        n_past = past_len // past_blk
        slab = past_blk * N_HEADS
        newest = pl.BlockSpec((None, slab, HEAD_DIM), lambda b, h: (b, n_past - 1, 0))
        hbm = pl.BlockSpec(memory_space=pl.ANY)
        in_specs += [newest, newest, hbm, hbm]
        kp = k_past.reshape(batch, past_len * N_HEADS, HEAD_DIM)
        vp = v_past.reshape(batch, past_len * N_HEADS, HEAD_DIM)
        args += [kp, vp, kp, vp]
        scratch = [pltpu.VMEM((slab, HEAD_DIM), F32), pltpu.VMEM((slab, HEAD_DIM), F32),
                   pltpu.SemaphoreType.DMA((2,))]
    return pl.pallas_call(
        functools.partial(_sb_body, blk, n_q, heads, past_blk, n_past),
        out_shape=jax.ShapeDtypeStruct((n, d), BF16),
        grid=(batch, N_HEADS // heads), in_specs=in_specs, out_specs=tok,
        scratch_shapes=scratch,
        compiler_params=_params("parallel", "parallel"), name="stick_breaking")(*args)


def _gelu_tanh(x):
    half = 0.5 * x
    t = jnp.tanh(x * (0.7978845608028654 + (0.7978845608028654 * 0.044715) * (x * x)))
    return half + half * t


def _gmlp_body(length, n_sub, emit_v, h_ref, w_ref, g_ref, b_ref, ws_ref, bs_ref, y_ref, *v_out):
    h = h_ref[...]
    d_c = w_ref.shape[1] // 2
    dg = d_c // GMLP_GROUPS
    i = lax.broadcasted_iota(jnp.int32, (length, length), 0)
    j = lax.broadcasted_iota(jnp.int32, (length, length), 1)
    mask = (j // CHUNK) <= (i // CHUNK)
    bias = bs_ref[...]

    vs = [_gelu_tanh(_dot(h, w_ref[:, d_c + g * dg:d_c + (g + 1) * dg])) for g in range(GMLP_GROUPS)]
    total = functools.reduce(jnp.add, [jnp.sum(v, axis=-1, keepdims=True) for v in vs])
    total_sq = functools.reduce(jnp.add, [jnp.sum(v * v, axis=-1, keepdims=True) for v in vs])
    mean = total / d_c
    inv = lax.rsqrt(total_sq / d_c - mean * mean + EPS)
    for g in range(GMLP_GROUPS):
        cols = slice(g * dg, (g + 1) * dg)
        vn = (vs[g] - mean) * inv * g_ref[:, cols] + b_ref[:, cols]
        if emit_v:
            v_out[0][:, cols] = vn
        vn = vn.astype(BF16)
        u = _gelu_tanh(_dot(h, w_ref[:, cols]))
        w = jnp.where(mask, ws_ref[g], 0.0).astype(BF16)
        for c in range(n_sub):
            rows = slice(c * length, (c + 1) * length)
            mixed = _dot(w, vn[rows]) + bias[:, g:g + 1]
            y_ref[rows, cols] = (u[rows] * mixed).astype(BF16)


def _gmlp(h, w_in, ln_g, ln_b, w_s, b_s, length, tile, emit_v):
    n, d = h.shape
    d_c = w_in[0].shape[-1] // 2
    bs = b_s.T
    out_shape = [jax.ShapeDtypeStruct((n, d_c), BF16)]
    out_specs = [_rows(tile, d_c)]
    if emit_v:
        out_shape.append(jax.ShapeDtypeStruct((n, d_c), F32))
        out_specs.append(_rows(tile, d_c))
    outs = pl.pallas_call(
        functools.partial(_gmlp_body, length, tile // length, emit_v),
        out_shape=out_shape, grid=(n // tile,),
        in_specs=[_rows(tile, d), w_in[1], ln_g[1], ln_b[1], _resident(w_s.shape), _resident(bs.shape)],
        out_specs=out_specs,
        compiler_params=_params("parallel"), name="gmlp")(h, w_in[0], ln_g[0], ln_b[0], w_s, bs)
    return (outs[0], outs[1]) if emit_v else (outs[0], None)


def _trunk(x3, p, hgrn_state, sb_k, sb_v, want_cv):
    batch, seq, d = x3.shape
    n = batch * seq
    tile = min(TOKEN_TILE, n)
    depth = p['ffn1_norm'].shape[0]
    x = x3.reshape(n, d)
    new_k, new_v, new_s, new_cv = [], [], [], []
    pre = []

    def row(name, l):
        v = p[name]
        return _layer(v.reshape(v.shape[0], 1, -1), l)

    for l in range(depth):
        x, h = _ffn(x, pre, row('ffn1_norm', l), _layer(p['ffn1_w_gate'], l), _layer(p['ffn1_w_up'], l),
                    _layer(p['ffn1_w_down'], l), row('mix_norm', l), tile)
        if l % 2 == 0:
            e = l // 2
            qa, lf, va, ga, qb, kb, vb, k, v = _abin(h, _layer(p['ab_w_in'], e), p['ab_lb'], row('ab_g_q', e),
                                                     row('ab_g_k', e), e, tile)
            mix_a, s_new = _hgrn(qa, lf, va, ga, None if hgrn_state is None else hgrn_state[e],
                                 p['ab_g_out'][e], p['ab_lb'], e, batch, seq)
            mix_b = _stick_breaking(qb, kb, vb, None if sb_k is None else sb_k[e],
                                    None if sb_v is None else sb_v[e], batch, seq)
            pre = [(mix_a, _layer(p['ab_w_out'], e, 0, 2)), (mix_b, _layer(p['ab_w_out'], e, 1, 2))]
            new_k.append(k.reshape(batch, seq, N_HEADS, HEAD_DIM))
            new_v.append(v.reshape(batch, seq, N_HEADS, HEAD_DIM))
            new_s.append(s_new)
        else:
            o = l // 2
            length = min(seq, GMLP_LEN)
            y, cv = _gmlp(h, _layer(p['c_w_in'], o), row('c_ln_g', o), row('c_ln_b', o),
                          p['c_w_s'][o, :, :length, :length], p['c_b_s'][o, :, :length], length, tile, want_cv)
            pre = [(y, _layer(p['c_w_out'], o))]
            if want_cv:
                new_cv.append(cv.reshape(batch, seq, -1))
        x, _ = _ffn(x, pre, row('ffn2_norm', l), _layer(p['ffn2_w_gate'], l), _layer(p['ffn2_w_up'], l),
                    _layer(p['ffn2_w_down'], l), None, tile)
        pre = []
    return x.reshape(batch, seq, d), new_k, new_v, new_s, new_cv


_MATMUL_WEIGHTS = ('ffn1_w_gate', 'ffn1_w_up', 'ffn1_w_down', 'ffn2_w_gate', 'ffn2_w_up',
                   'ffn2_w_down', 'ab_w_in', 'ab_w_out', 'c_w_in', 'c_w_out')


def kernel(x_prompt, x_sample, cache_sb_k, cache_sb_v, state_hgrn, ffn1_norm, ffn1_w_gate, ffn1_w_up, ffn1_w_down, mix_norm, ffn2_norm, ffn2_w_gate, ffn2_w_up, ffn2_w_down, ab_w_in, ab_lb, ab_g_out, ab_g_q, ab_g_k, ab_w_out, c_w_in, c_ln_g, c_ln_b, c_w_s, c_b_s, c_w_out):
    p = {'ffn1_norm': ffn1_norm, 'ffn1_w_gate': ffn1_w_gate, 'ffn1_w_up': ffn1_w_up, 'ffn1_w_down': ffn1_w_down,
         'mix_norm': mix_norm, 'ffn2_norm': ffn2_norm, 'ffn2_w_gate': ffn2_w_gate, 'ffn2_w_up': ffn2_w_up,
         'ffn2_w_down': ffn2_w_down, 'ab_w_in': ab_w_in, 'ab_lb': ab_lb, 'ab_g_out': ab_g_out, 'ab_g_q': ab_g_q,
         'ab_g_k': ab_g_k, 'ab_w_out': ab_w_out, 'c_w_in': c_w_in, 'c_ln_g': c_ln_g, 'c_ln_b': c_ln_b,
         'c_w_s': c_w_s, 'c_b_s': c_b_s, 'c_w_out': c_w_out}
    for name in _MATMUL_WEIGHTS:
        p[name] = p[name].astype(BF16)
    y_prompt, pk, pv, ps, _ = _trunk(x_prompt, p, None, None, None, False)
    y_sample, sk, sv, ss, scv = _trunk(x_sample, p, state_hgrn, cache_sb_k, cache_sb_v, True)
    return (y_prompt, y_sample, jnp.stack(pk), jnp.stack(pv), jnp.stack(ps),
            jnp.stack(sk), jnp.stack(sv), jnp.stack(ss), jnp.stack(scv))
```

```python
import functools

import jax
import jax.numpy as jnp
from jax import lax
from jax.experimental import pallas as pl
from jax.experimental.pallas import tpu as pltpu

F32 = jnp.float32
BF16 = jnp.bfloat16

EPS = 1e-6
FFN_SCALE = 0.5
HEAD_DIM = 128
N_HEADS = 8
CHUNK = 64
GMLP_LEN = 128
GMLP_GROUPS = 8
V7X_VMEM_LIMIT_BYTES = 56 * 1024 * 1024
TOKEN_TILE = 512
FFN_ROW_GROUPS = 2
AB_ROW_GROUPS = 2
HGRN_CHUNK = 128
HGRN_HEADS_PER_STEP = 4
HGRN_SAFE_EXPONENT = 60.0
SB_BLOCK = 256
SB_HEADS_PER_STEP = 4
SB_EXIT = -105.0


def _params(*sem):
    return pltpu.CompilerParams(dimension_semantics=sem, vmem_limit_bytes=V7X_VMEM_LIMIT_BYTES)


def _resident(shape):
    zeros = (0,) * len(shape)
    return pl.BlockSpec(shape, lambda *_: zeros, pipeline_mode=pl.Buffered(1))


def _layer(stacked, layer, row_block=0, n_row_blocks=1):
    rows = stacked.shape[1] // n_row_blocks
    block = (None, rows) + stacked.shape[2:]
    index = (layer, row_block) + (0,) * (stacked.ndim - 2)
    return stacked, pl.BlockSpec(block, lambda *_: index, pipeline_mode=pl.Buffered(1))


def _rows(tile, width):
    return pl.BlockSpec((tile, width), lambda i: (i, 0))


def _sigmoid(x):
    return 1.0 / (1.0 + jnp.exp(-x))


def _rms_rows(x, g):
    ms = jnp.mean(x * x, axis=-1, keepdims=True)
    return x * lax.rsqrt(ms + EPS) * g


def _dot(a, b):
    return jnp.dot(a, b, preferred_element_type=F32)


def _dot_nt(a, b):
    return lax.dot_general(a, b, (((1,), (1,)), ((), ())), preferred_element_type=F32)


def _dot_tn(a, b):
    return lax.dot_general(a, b, (((0,), (0,)), ((), ())), preferred_element_type=F32)


def _split_bf16(x):
    hi = x.astype(BF16)
    lo = (x - hi.astype(F32)).astype(BF16)
    return hi, lo


def _head_rows(head, n):
    return pl.ds(head, n, stride=N_HEADS)


def _ffn_body(n_pre, has_next, *refs):
    x_ref = refs[0]
    pre = refs[1:1 + 2 * n_pre]
    g_ref, wg_ref, wu_ref, wd_ref = refs[1 + 2 * n_pre:5 + 2 * n_pre]
    pos = 5 + 2 * n_pre
    g2_ref = refs[pos] if has_next else None
    pos += int(has_next)
    xo_ref = refs[pos]
    ho_ref = refs[pos + 1] if has_next else None

    tile = x_ref.shape[0]
    parts = [slice(k * (tile // FFN_ROW_GROUPS), (k + 1) * (tile // FFN_ROW_GROUPS)) for k in range(FFN_ROW_GROUPS)]
    xs = [x_ref[r, :] for r in parts]
    for i in range(n_pre):
        xs = [x + _dot(pre[2 * i][r, :], pre[2 * i + 1][...]) for x, r in zip(xs, parts)]
    hs = [_rms_rows(x, g_ref[...]).astype(BF16) for x in xs]
    gates = [_dot(h, wg_ref[...]) for h in hs]
    ups = [_dot(h, wu_ref[...]) for h in hs]
    acts = [(a * _sigmoid(a) * b).astype(BF16) for a, b in zip(gates, ups)]
    downs = [_dot(act, wd_ref[...]) for act in acts]
    for x, y, r in zip(xs, downs, parts):
        x = x + FFN_SCALE * y
        xo_ref[r, :] = x
        if has_next:
            ho_ref[r, :] = _rms_rows(x, g2_ref[...]).astype(BF16)


def _ffn(x, pre, g, wg, wu, wd, g_next, tile):
    n, d = x.shape
    has_next = g_next is not None
    in_specs = [_rows(tile, d)]
    args = [x]
    for m, (w, w_spec) in pre:
        in_specs += [_rows(tile, m.shape[1]), w_spec]
        args += [m, w]
    for arr, spec in (g, wg, wu, wd):
        in_specs.append(spec)
        args.append(arr)
    out_shape = [jax.ShapeDtypeStruct((n, d), F32)]
    out_specs = [_rows(tile, d)]
    if has_next:
        in_specs.append(g_next[1])
        args.append(g_next[0])
        out_shape.append(jax.ShapeDtypeStruct((n, d), BF16))
        out_specs.append(_rows(tile, d))
    outs = pl.pallas_call(
        functools.partial(_ffn_body, len(pre), has_next),
        out_shape=out_shape, grid=(n // tile,), in_specs=in_specs, out_specs=out_specs,
        compiler_params=_params("parallel"), name="ffn")(*args)
    return (outs[0], outs[1]) if has_next else (outs[0], None)


def _abin_body(layer, h_ref, w_ref, lbp_ref, gq_ref, gk_ref,
               qa_ref, lf_ref, va_ref, ga_ref, qb_ref, kb_ref, vb_ref, k_ref, v_ref):
    tile, d = h_ref.shape
    group = tile // AB_ROW_GROUPS

    lbp = lbp_ref[...]
    e = jnp.exp(lbp - jnp.max(lbp, axis=0, keepdims=True))
    lb = jnp.sum(e[:layer + 1], axis=0, keepdims=True) / jnp.sum(e, axis=0, keepdims=True)

    for k in range(AB_ROW_GROUPS):
        r = slice(k * group, (k + 1) * group)
        h = h_ref[r, :]

        def seg(i, h=h):
            return _dot(h, w_ref[:, i * d:(i + 1) * d])

        qa = seg(0)
        qa_ref[r, :] = (qa * _sigmoid(qa)).astype(BF16)
        f = lb + (1.0 - lb) * _sigmoid(seg(1))
        lf_ref[r, :] = jnp.log(f)
        va_ref[r, :] = seg(2).astype(BF16)
        ga_ref[r, :] = _sigmoid(seg(3)).astype(BF16)

        qb = seg(4)
        kb = seg(5)
        vb = seg(6)
        vb_ref[r, :] = vb.astype(BF16)
        for hh in range(N_HEADS):
            sl = slice(hh * HEAD_DIM, (hh + 1) * HEAD_DIM)
            qb_ref[r, sl] = (_rms_rows(qb[:, sl], gq_ref[...]) * HEAD_DIM ** -0.5).astype(BF16)
            kn = _rms_rows(kb[:, sl], gk_ref[...])
            kb_ref[r, sl] = kn.astype(BF16)
            head_rows = pl.ds(k * group * N_HEADS + hh, group, stride=N_HEADS)
            k_ref[head_rows, :] = kn
            v_ref[head_rows, :] = vb[:, sl]


def _abin(h, w_in, lb_param, g_q, g_k, layer, tile):
    n, d = h.shape
    assert d == N_HEADS * HEAD_DIM
    feat = [BF16, F32, BF16, BF16, BF16, BF16, BF16]
    out_shape = [jax.ShapeDtypeStruct((n, d), t) for t in feat]
    out_specs = [_rows(tile, d) for _ in feat]
    for _ in range(2):
        out_shape.append(jax.ShapeDtypeStruct((n * N_HEADS, HEAD_DIM), F32))
        out_specs.append(_rows(tile * N_HEADS, HEAD_DIM))
    return pl.pallas_call(
        functools.partial(_abin_body, layer),
        out_shape=out_shape, grid=(n // tile,),
        in_specs=[_rows(tile, d), w_in[1], _resident(lb_param.shape), g_q[1], g_k[1]],
        out_specs=out_specs,
        compiler_params=_params("parallel"), name="ab_in_proj")(
            h, w_in[0], lb_param, g_q[0], g_k[0])


def _hgrn_scores_any_decay(q, k, lf2, chunk):
    row = lax.broadcasted_iota(jnp.int32, (chunk, chunk), 0)
    col = lax.broadcasted_iota(jnp.int32, (chunk, chunk), 1)
    t1 = lax.broadcasted_iota(jnp.int32, (chunk, 1), 0)
    scores = jnp.where(row == col, jnp.sum(q * k, axis=-1, keepdims=True), 0.0)
    half = chunk // 2
    while half >= 1:
        blk = 2 * half
        same = (row // blk) == (col // blk)
        mid = (row // blk) * blk + half
        right = row >= mid
        seg = same & ((right & (col >= mid) & (col <= row)) | ((col > row) & (col < mid)))
        d2 = _dot(jnp.where(seg, 1.0, 0.0).astype(BF16), lf2)
        decay = jnp.exp(d2[:, :HEAD_DIM] + d2[:, HEAD_DIM:])
        right1 = (t1 % blk) >= half
        qe = jnp.where(right1, q * decay, 0.0).astype(BF16)
        ke = jnp.where(right1, 0.0, k * decay).astype(BF16)
        scores = scores + jnp.where(same, _dot_nt(qe, ke), 0.0)
        half //= 2
    return scores


def _hgrn_body(chunk, n_chunks, heads, has_s0, factored, *refs):
    q_ref, lf_ref, v_ref, gt_ref = refs[:4]
    pos = 4
    s0_ref = refs[pos] if has_s0 else None
    pos += int(has_s0)
    go_ref, o_ref, so_ref = refs[pos:pos + 3]

    row = lax.broadcasted_iota(jnp.int32, (chunk, chunk), 0)
    col = lax.broadcasted_iota(jnp.int32, (chunk, chunk), 1)
    causal = col <= row
    lower = jnp.where(causal, 1.0, 0.0).astype(BF16)
    mid = chunk // 2
    head_cols = [slice(j * HEAD_DIM, (j + 1) * HEAD_DIM) for j in range(heads)]

    def step(i, states):
        rows = pl.ds(pl.multiple_of(i * chunk, chunk), chunk)
        lfs = [lf_ref[rows, cl] for cl in head_cols]
        lf2s = [jnp.concatenate(_split_bf16(lf), axis=1) for lf in lfs]
        c2s = [_dot(lower, lf2) for lf2 in lf2s]
        qs = [q_ref[rows, cl].astype(F32) for cl in head_cols]
        ks = [1.0 - jnp.exp(lf) for lf in lfs]
        cs = [c2[:, :HEAD_DIM] + c2[:, HEAD_DIM:] for c2 in c2s]
        c_lasts = [c[chunk - 1:chunk, :] for c in cs]
        vs = [v_ref[rows, cl] for cl in head_cols]
        if factored:
            c_mids = [c[mid - 1:mid, :] for c in cs]
            rel = [c - cm for c, cm in zip(cs, c_mids)]
            qes = [(q * jnp.exp(a)).astype(BF16) for q, a in zip(qs, rel)]
            kes = [(k * jnp.exp(-a)).astype(BF16) for k, a in zip(ks, rel)]
            rhs = [jnp.concatenate([ke, (st * jnp.exp(cm)).astype(BF16)], axis=0)
                   for ke, st, cm in zip(kes, states, c_mids)]
            both = [_dot_nt(qe, r) for qe, r in zip(qes, rhs)]
            grown = [_dot_tn(v, ke) * jnp.exp(cl - cm) for v, ke, cl, cm in zip(vs, kes, c_lasts, c_mids)]
            scores = [jnp.where(causal, bt[:, :chunk], 0.0) for bt in both]
            inter = [bt[:, chunk:] for bt in both]
        else:
            inter = [_dot_nt((q * jnp.exp(c)).astype(BF16), st.astype(BF16)) for q, c, st in zip(qs, cs, states)]
            scores = [_hgrn_scores_any_decay(q, k, lf2, chunk) for q, k, lf2 in zip(qs, ks, lf2s)]
            kds = [(k * jnp.exp(cl - c)).astype(BF16) for k, cl, c in zip(ks, c_lasts, cs)]
            grown = [_dot_tn(v, kd) for v, kd in zip(vs, kds)]
        intra = [_dot(s.astype(BF16), v) for s, v in zip(scores, vs)]
        for j, cl in enumerate(head_cols):
            o = _rms_rows(inter[j] + intra[j], go_ref[j]) * gt_ref[rows, cl].astype(F32)
            o_ref[rows, cl] = o.astype(o_ref.dtype)
        return tuple(st * jnp.exp(cl) + g for st, cl, g in zip(states, c_lasts, grown))

    if has_s0:
        states = tuple(s0_ref[j].T for j in range(heads))
    else:
        states = tuple(jnp.zeros((HEAD_DIM, HEAD_DIM), F32) for _ in range(heads))
    states = lax.fori_loop(0, n_chunks, step, states, unroll=4) if n_chunks > 1 else step(0, states)
    for j in range(heads):
        so_ref[j] = states[j].T


def _hgrn_call(qa, lf, va, ga, s0, g_out, batch, seq, chunk, heads, factored):
    n, d = qa.shape
    has_s0 = s0 is not None
    width = heads * HEAD_DIM
    tok = pl.BlockSpec((seq, width), lambda b, h: (b, h))
    state = pl.BlockSpec((None, heads, HEAD_DIM, HEAD_DIM), lambda b, h: (b, h, 0, 0))
    in_specs = [tok, tok, tok, tok]
    args = [qa, lf, va, ga]
    if has_s0:
        in_specs.append(state)
        args.append(s0)
    in_specs.append(pl.BlockSpec((heads, 1, HEAD_DIM), lambda b, h: (h, 0, 0)))
    args.append(g_out.reshape(N_HEADS, 1, HEAD_DIM))
    return pl.pallas_call(
        functools.partial(_hgrn_body, chunk, seq // chunk, heads, has_s0, factored),
        out_shape=[jax.ShapeDtypeStruct((n, d), BF16),
                   jax.ShapeDtypeStruct((batch, N_HEADS, HEAD_DIM, HEAD_DIM), F32)],
        grid=(batch, N_HEADS // heads), in_specs=in_specs, out_specs=[tok, state],
        compiler_params=_params("parallel", "parallel"), name="hgrn2")(*args)


def _hgrn(qa, lf, va, ga, s0, g_out, lb_param, layer, batch, seq):
    chunk = min(seq, HGRN_CHUNK)
    heads = N_HEADS if seq < HGRN_CHUNK else HGRN_HEADS_PER_STEP
    call = functools.partial(_hgrn_call, qa, lf, va, ga, s0, g_out, batch, seq, chunk, heads)
    lb = jnp.cumsum(jax.nn.softmax(lb_param.astype(F32), axis=0), axis=0)[layer]
    reach = (chunk // 2) * jnp.max(-jnp.log(lb))
    return lax.cond(reach <= HGRN_SAFE_EXPONENT, lambda: call(True), lambda: call(False))


def _sb_logits_many(items):
    zs = [_dot_nt(q, kb) for q, kb, _, _ in items]
    sps = [jnp.maximum(z, 0.0) + jnp.log(1.0 + jnp.exp(-jnp.abs(z))) for z in zs]
    costs = [sp if it[3] is None else jnp.where(it[3], sp, 0.0) for sp, it in zip(sps, items)]
    suffix = [_dot(c.astype(BF16), it[2]) for c, it in zip(costs, items)]
    return [(z - sp - sf, jnp.sum(c, axis=-1, keepdims=True)) for z, sp, sf, c in zip(zs, sps, suffix, costs)]


def _sb_accumulate(logits, cost, vb, mask, acc, spent):
    w = jnp.exp(logits - spent)
    if mask is not None:
        w = jnp.where(mask, w, 0.0)
    return acc + _dot(w.astype(BF16), vb), spent + cost


def _strict_upper(n):
    j = lax.broadcasted_iota(jnp.int32, (n, n), 0)
    k = lax.broadcasted_iota(jnp.int32, (n, n), 1)
    return jnp.where(j > k, 1.0, 0.0).astype(BF16)


def _sb_body(blk, n_q, heads, past_blk, n_past, *refs):
    q_ref, k_ref, v_ref = refs[:3]
    if n_past:
        kp_new_ref, vp_new_ref, kp_hbm, vp_hbm, o_ref, kbuf, vbuf, sem = refs[3:]
    else:
        o_ref = refs[3]

    t = lax.broadcasted_iota(jnp.int32, (blk, blk), 0)
    s = lax.broadcasted_iota(jnp.int32, (blk, blk), 1)
    diag_mask = s < t
    upper = _strict_upper(blk)
    upper_past = _strict_upper(past_blk) if n_past else None
    head_cols = [slice(j * HEAD_DIM, (j + 1) * HEAD_DIM) for j in range(heads)]

    def still_live(spent):
        return functools.reduce(jnp.minimum, [jnp.min(r) for r in spent]) < -SB_EXIT

    for qi in range(n_q):
        rows = pl.ds(qi * blk, blk)
        qs = [q_ref[rows, c] for c in head_cols]
        n_older = qi + n_past
        items = [(qs[j], k_ref[rows, c], upper, diag_mask) for j, c in enumerate(head_cols)]
        values = [(v_ref[rows, c], diag_mask) for c in head_cols]
        if qi > 0:
            prev = pl.ds((qi - 1) * blk, blk)
            items += [(qs[j], k_ref[prev, c], upper, None) for j, c in enumerate(head_cols)]
            values += [(v_ref[prev, c], None) for c in head_cols]
        elif n_past:
            items += [(qs[j], kp_new_ref[_head_rows(j, past_blk), :].astype(BF16), upper_past, None)
                      for j in range(heads)]
            values += [(vp_new_ref[_head_rows(j, past_blk), :].astype(BF16), None) for j in range(heads)]
        logits = _sb_logits_many(items)
        accs = [jnp.zeros((blk, HEAD_DIM), F32) for _ in range(heads)]
        runs = [jnp.zeros((blk, 1), F32) for _ in range(heads)]
        for n, ((lg, stay), (vb, mask)) in enumerate(zip(logits, values)):
            j = n % heads
            accs[j], runs[j] = _sb_accumulate(lg, stay, vb, mask, accs[j], runs[j])

        if n_older > 1:
            def cond(carry):
                return (carry[0] < n_older) & carry[1]

            def body(carry, qi=qi, qs=qs):
                idx = carry[0]
                accs, runs = list(carry[2]), list(carry[3])
                if n_past:
                    src = pl.ds(pl.multiple_of((n_past - 1 - idx) * past_blk * N_HEADS, past_blk * N_HEADS),
                                past_blk * N_HEADS)
                    b = pl.program_id(0)
                    copies = [pltpu.make_async_copy(kp_hbm.at[b, src, :], kbuf, sem.at[0]),
                              pltpu.make_async_copy(vp_hbm.at[b, src, :], vbuf, sem.at[1])]
                    for cp in copies:
                        cp.start()
                    for cp in copies:
                        cp.wait()
                if n_past:
                    items = [(qs[j], kbuf[_head_rows(j, past_blk), :].astype(BF16), upper_past, None)
                             for j in range(heads)]
                    vbs = [vbuf[_head_rows(j, past_blk), :].astype(BF16) for j in range(heads)]
                else:
                    kr = pl.ds(pl.multiple_of((qi - 1 - idx) * blk, blk), blk)
                    items = [(qs[j], k_ref[kr, c], upper, None) for j, c in enumerate(head_cols)]
                    vbs = [v_ref[kr, c] for c in head_cols]
                for j, (lg, stay) in enumerate(_sb_logits_many(items)):
                    accs[j], runs[j] = _sb_accumulate(lg, stay, vbs[j], None, accs[j], runs[j])
                return idx + 1, still_live(runs), tuple(accs), tuple(runs)

            _, _, accs, _ = lax.while_loop(cond, body, (jnp.int32(1), still_live(runs), tuple(accs), tuple(runs)))
        for j, c in enumerate(head_cols):
            o_ref[rows, c] = accs[j].astype(o_ref.dtype)


def _stick_breaking(qn, kb, vb, k_past, v_past, batch, seq):
    n, d = qn.shape
    blk = min(seq, SB_BLOCK)
    n_q = seq // blk
    heads = N_HEADS if seq < SB_BLOCK else SB_HEADS_PER_STEP
    tok = pl.BlockSpec((seq, heads * HEAD_DIM), lambda b, h: (b, h))
    in_specs = [tok, tok, tok]
    args = [qn, kb, vb]
    scratch = []
    past_blk = n_past = 0
    if k_past is not None:
        assert n_q == 1 and heads == N_HEADS, "cached keys are only supported for a single query block"
        past_len = k_past.shape[1]
        past_blk = min(past_len, SB_BLOCK)
        n_past = past_len // past_blk
        slab = past_blk * N_HEADS
        newest = pl.BlockSpec((None, slab, HEAD_DIM), lambda b, h: (b, n_past - 1, 0))
        hbm = pl.BlockSpec(memory_space=pl.ANY)
        in_specs += [newest, newest, hbm, hbm]
        kp = k_past.reshape(batch, past_len * N_HEADS, HEAD_DIM)
        vp = v_past.reshape(batch, past_len * N_HEADS, HEAD_DIM)
        args += [kp, vp, kp, vp]
        scratch = [pltpu.VMEM((slab, HEAD_DIM), F32), pltpu.VMEM((slab, HEAD_DIM), F32),
                   pltpu.SemaphoreType.DMA((2,))]
    return pl.pallas_call(
        functools.partial(_sb_body, blk, n_q, heads, past_blk, n_past),
        out_shape=jax.ShapeDtypeStruct((n, d), BF16),
        grid=(batch, N_HEADS // heads), in_specs=in_specs, out_specs=tok,
        scratch_shapes=scratch,
        compiler_params=_params("parallel", "parallel"), name="stick_breaking")(*args)


def _gelu_tanh(x):
    half = 0.5 * x
    t = jnp.tanh(x * (0.7978845608028654 + (0.7978845608028654 * 0.044715) * (x * x)))
    return half + half * t


def _gmlp_body(length, n_sub, emit_v, h_ref, w_ref, g_ref, b_ref, ws_ref, bs_ref, y_ref, *v_out):
    h = h_ref[...]
    d_c = w_ref.shape[1] // 2
    dg = d_c // GMLP_GROUPS
    i = lax.broadcasted_iota(jnp.int32, (length, length), 0)
    j = lax.broadcasted_iota(jnp.int32, (length, length), 1)
    mask = (j // CHUNK) <= (i // CHUNK)
    bias = bs_ref[...]

    vs = [_gelu_tanh(_dot(h, w_ref[:, d_c + g * dg:d_c + (g + 1) * dg])) for g in range(GMLP_GROUPS)]
    total = functools.reduce(jnp.add, [jnp.sum(v, axis=-1, keepdims=True) for v in vs])
    total_sq = functools.reduce(jnp.add, [jnp.sum(v * v, axis=-1, keepdims=True) for v in vs])
    mean = total / d_c
    inv = lax.rsqrt(total_sq / d_c - mean * mean + EPS)
    for g in range(GMLP_GROUPS):
        cols = slice(g * dg, (g + 1) * dg)
        vn = (vs[g] - mean) * inv * g_ref[:, cols] + b_ref[:, cols]
        if emit_v:
            v_out[0][:, cols] = vn
        vn = vn.astype(BF16)
        u = _gelu_tanh(_dot(h, w_ref[:, cols]))
        w = jnp.where(mask, ws_ref[g], 0.0).astype(BF16)
        for c in range(n_sub):
            rows = slice(c * length, (c + 1) * length)
            mixed = _dot(w, vn[rows]) + bias[:, g:g + 1]
            y_ref[rows, cols] = (u[rows] * mixed).astype(BF16)


def _gmlp(h, w_in, ln_g, ln_b, w_s, b_s, length, tile, emit_v):
    n, d = h.shape
    d_c = w_in[0].shape[-1] // 2
    bs = b_s.T
    out_shape = [jax.ShapeDtypeStruct((n, d_c), BF16)]
    out_specs = [_rows(tile, d_c)]
    if emit_v:
        out_shape.append(jax.ShapeDtypeStruct((n, d_c), F32))
        out_specs.append(_rows(tile, d_c))
    outs = pl.pallas_call(
        functools.partial(_gmlp_body, length, tile // length, emit_v),
        out_shape=out_shape, grid=(n // tile,),
        in_specs=[_rows(tile, d), w_in[1], ln_g[1], ln_b[1], _resident(w_s.shape), _resident(bs.shape)],
        out_specs=out_specs,
        compiler_params=_params("parallel"), name="gmlp")(h, w_in[0], ln_g[0], ln_b[0], w_s, bs)
    return (outs[0], outs[1]) if emit_v else (outs[0], None)


def _trunk(x3, p, hgrn_state, sb_k, sb_v, want_cv):
    batch, seq, d = x3.shape
    n = batch * seq
    tile = min(TOKEN_TILE, n)
    depth = p['ffn1_norm'].shape[0]
    x = x3.reshape(n, d)
    new_k, new_v, new_s, new_cv = [], [], [], []
    pre = []

    def row(name, l):
        v = p[name]
        return _layer(v.reshape(v.shape[0], 1, -1), l)

    for l in range(depth):
        x, h = _ffn(x, pre, row('ffn1_norm', l), _layer(p['ffn1_w_gate'], l), _layer(p['ffn1_w_up'], l),
                    _layer(p['ffn1_w_down'], l), row('mix_norm', l), tile)
        if l % 2 == 0:
            e = l // 2
            qa, lf, va, ga, qb, kb, vb, k, v = _abin(h, _layer(p['ab_w_in'], e), p['ab_lb'], row('ab_g_q', e),
                                                     row('ab_g_k', e), e, tile)
            mix_a, s_new = _hgrn(qa, lf, va, ga, None if hgrn_state is None else hgrn_state[e],
                                 p['ab_g_out'][e], p['ab_lb'], e, batch, seq)
            mix_b = _stick_breaking(qb, kb, vb, None if sb_k is None else sb_k[e],
                                    None if sb_v is None else sb_v[e], batch, seq)
            pre = [(mix_a, _layer(p['ab_w_out'], e, 0, 2)), (mix_b, _layer(p['ab_w_out'], e, 1, 2))]
            new_k.append(k.reshape(batch, seq, N_HEADS, HEAD_DIM))
            new_v.append(v.reshape(batch, seq, N_HEADS, HEAD_DIM))
            new_s.append(s_new)
        else:
            o = l // 2
            length = min(seq, GMLP_LEN)
            y, cv = _gmlp(h, _layer(p['c_w_in'], o), row('c_ln_g', o), row('c_ln_b', o),
                          p['c_w_s'][o, :, :length, :length], p['c_b_s'][o, :, :length], length, tile, want_cv)
            pre = [(y, _layer(p['c_w_out'], o))]
            if want_cv:
                new_cv.append(cv.reshape(batch, seq, -1))
        x, _ = _ffn(x, pre, row('ffn2_norm', l), _layer(p['ffn2_w_gate'], l), _layer(p['ffn2_w_up'], l),
                    _layer(p['ffn2_w_down'], l), None, tile)
        pre = []
    return x.reshape(batch, seq, d), new_k, new_v, new_s, new_cv


_MATMUL_WEIGHTS = ('ffn1_w_gate', 'ffn1_w_up', 'ffn1_w_down', 'ffn2_w_gate', 'ffn2_w_up',
                   'ffn2_w_down', 'ab_w_in', 'ab_w_out', 'c_w_in', 'c_w_out')


def kernel(x_prompt, x_sample, cache_sb_k, cache_sb_v, state_hgrn, ffn1_norm, ffn1_w_gate, ffn1_w_up, ffn1_w_down, mix_norm, ffn2_norm, ffn2_w_gate, ffn2_w_up, ffn2_w_down, ab_w_in, ab_lb, ab_g_out, ab_g_q, ab_g_k, ab_w_out, c_w_in, c_ln_g, c_ln_b, c_w_s, c_b_s, c_w_out):
    p = {'ffn1_norm': ffn1_norm, 'ffn1_w_gate': ffn1_w_gate, 'ffn1_w_up': ffn1_w_up, 'ffn1_w_down': ffn1_w_down,
         'mix_norm': mix_norm, 'ffn2_norm': ffn2_norm, 'ffn2_w_gate': ffn2_w_gate, 'ffn2_w_up': ffn2_w_up,
         'ffn2_w_down': ffn2_w_down, 'ab_w_in': ab_w_in, 'ab_lb': ab_lb, 'ab_g_out': ab_g_out, 'ab_g_q': ab_g_q,
         'ab_g_k': ab_g_k, 'ab_w_out': ab_w_out, 'c_w_in': c_w_in, 'c_ln_g': c_ln_g, 'c_ln_b': c_ln_b,
         'c_w_s': c_w_s, 'c_b_s': c_b_s, 'c_w_out': c_w_out}
    for name in _MATMUL_WEIGHTS:
        p[name] = p[name].astype(BF16)
    y_prompt, pk, pv, ps, _ = _trunk(x_prompt, p, None, None, None, False)
    y_sample, sk, sv, ss, scv = _trunk(x_sample, p, state_hgrn, cache_sb_k, cache_sb_v, True)
    return (y_prompt, y_sample, jnp.stack(pk), jnp.stack(pv), jnp.stack(ps),
            jnp.stack(sk), jnp.stack(sv), jnp.stack(ss), jnp.stack(scv))
```

```python
import functools

import jax
import jax.numpy as jnp
from jax import lax
from jax.experimental import pallas as pl
from jax.experimental.pallas import tpu as pltpu

F32 = jnp.float32
BF16 = jnp.bfloat16

EPS = 1e-6
FFN_SCALE = 0.5
HEAD_DIM = 128
N_HEADS = 8
CHUNK = 64
GMLP_LEN = 128
GMLP_GROUPS = 8
V7X_VMEM_LIMIT_BYTES = 56 * 1024 * 1024
TOKEN_TILE = 512
FFN_ROW_GROUPS = 2
AB_ROW_GROUPS = 2
HGRN_CHUNK = 128
HGRN_HEADS_PER_STEP = 4
HGRN_SAFE_EXPONENT = 60.0
SB_BLOCK = 256
SB_HEADS_PER_STEP = 4
SB_EXIT = -105.0


def _params(*sem):
    return pltpu.CompilerParams(dimension_semantics=sem, vmem_limit_bytes=V7X_VMEM_LIMIT_BYTES)


def _resident(shape):
    zeros = (0,) * len(shape)
    return pl.BlockSpec(shape, lambda *_: zeros, pipeline_mode=pl.Buffered(1))


def _layer(stacked, layer, row_block=0, n_row_blocks=1):
    rows = stacked.shape[1] // n_row_blocks
    block = (None, rows) + stacked.shape[2:]
    index = (layer, row_block) + (0,) * (stacked.ndim - 2)
    return stacked, pl.BlockSpec(block, lambda *_: index, pipeline_mode=pl.Buffered(1))


def _rows(tile, width):
    return pl.BlockSpec((tile, width), lambda i: (i, 0))


def _sigmoid(x):
    return 1.0 / (1.0 + jnp.exp(-x))


def _rms_rows(x, g):
    ms = jnp.mean(x * x, axis=-1, keepdims=True)
    return x * lax.rsqrt(ms + EPS) * g


def _dot(a, b):
    return jnp.dot(a, b, preferred_element_type=F32)


def _dot_nt(a, b):
    return lax.dot_general(a, b, (((1,), (1,)), ((), ())), preferred_element_type=F32)


def _dot_tn(a, b):
    return lax.dot_general(a, b, (((0,), (0,)), ((), ())), preferred_element_type=F32)


def _split_bf16(x):
    hi = x.astype(BF16)
    lo = (x - hi.astype(F32)).astype(BF16)
    return hi, lo


def _head_rows(head, n):
    return pl.ds(head, n, stride=N_HEADS)


def _ffn_body(n_pre, has_next, *refs):
    x_ref = refs[0]
    pre = refs[1:1 + 2 * n_pre]
    g_ref, wg_ref, wu_ref, wd_ref = refs[1 + 2 * n_pre:5 + 2 * n_pre]
    pos = 5 + 2 * n_pre
    g2_ref = refs[pos] if has_next else None
    pos += int(has_next)
    xo_ref = refs[pos]
    ho_ref = refs[pos + 1] if has_next else None

    tile = x_ref.shape[0]
    parts = [slice(k * (tile // FFN_ROW_GROUPS), (k + 1) * (tile // FFN_ROW_GROUPS)) for k in range(FFN_ROW_GROUPS)]
    xs = [x_ref[r, :] for r in parts]
    for i in range(n_pre):
        xs = [x + _dot(pre[2 * i][r, :], pre[2 * i + 1][...]) for x, r in zip(xs, parts)]
    hs = [_rms_rows(x, g_ref[...]).astype(BF16) for x in xs]
    gates = [_dot(h, wg_ref[...]) for h in hs]
    ups = [_dot(h, wu_ref[...]) for h in hs]
    acts = [(a * _sigmoid(a) * b).astype(BF16) for a, b in zip(gates, ups)]
    downs = [_dot(act, wd_ref[...]) for act in acts]
    for x, y, r in zip(xs, downs, parts):
        x = x + FFN_SCALE * y
        xo_ref[r, :] = x
        if has_next:
            ho_ref[r, :] = _rms_rows(x, g2_ref[...]).astype(BF16)


def _ffn(x, pre, g, wg, wu, wd, g_next, tile):
    n, d = x.shape
    has_next = g_next is not None
    in_specs = [_rows(tile, d)]
    args = [x]
    for m, (w, w_spec) in pre:
        in_specs += [_rows(tile, m.shape[1]), w_spec]
        args += [m, w]
    for arr, spec in (g, wg, wu, wd):
        in_specs.append(spec)
        args.append(arr)
    out_shape = [jax.ShapeDtypeStruct((n, d), F32)]
    out_specs = [_rows(tile, d)]
    if has_next:
        in_specs.append(g_next[1])
        args.append(g_next[0])
        out_shape.append(jax.ShapeDtypeStruct((n, d), BF16))
        out_specs.append(_rows(tile, d))
    outs = pl.pallas_call(
        functools.partial(_ffn_body, len(pre), has_next),
        out_shape=out_shape, grid=(n // tile,), in_specs=in_specs, out_specs=out_specs,
        compiler_params=_params("parallel"), name="ffn")(*args)
    return (outs[0], outs[1]) if has_next else (outs[0], None)


def _abin_body(layer, h_ref, w_ref, lbp_ref, gq_ref, gk_ref,
               qa_ref, lf_ref, va_ref, ga_ref, qb_ref, kb_ref, vb_ref, k_ref, v_ref):
    tile, d = h_ref.shape
    group = tile // AB_ROW_GROUPS

    lbp = lbp_ref[...]
    e = jnp.exp(lbp - jnp.max(lbp, axis=0, keepdims=True))
    lb = jnp.sum(e[:layer + 1], axis=0, keepdims=True) / jnp.sum(e, axis=0, keepdims=True)

    for k in range(AB_ROW_GROUPS):
        r = slice(k * group, (k + 1) * group)
        h = h_ref[r, :]

        def seg(i, h=h):
            return _dot(h, w_ref[:, i * d:(i + 1) * d])

        qa = seg(0)
        qa_ref[r, :] = (qa * _sigmoid(qa)).astype(BF16)
        f = lb + (1.0 - lb) * _sigmoid(seg(1))
        lf_ref[r, :] = jnp.log(f)
        va_ref[r, :] = seg(2).astype(BF16)
        ga_ref[r, :] = _sigmoid(seg(3)).astype(BF16)

        qb = seg(4)
        kb = seg(5)
        vb = seg(6)
        vb_ref[r, :] = vb.astype(BF16)
        for hh in range(N_HEADS):
            sl = slice(hh * HEAD_DIM, (hh + 1) * HEAD_DIM)
            qb_ref[r, sl] = (_rms_rows(qb[:, sl], gq_ref[...]) * HEAD_DIM ** -0.5).astype(BF16)
            kn = _rms_rows(kb[:, sl], gk_ref[...])
            kb_ref[r, sl] = kn.astype(BF16)
            head_rows = pl.ds(k * group * N_HEADS + hh, group, stride=N_HEADS)
            k_ref[head_rows, :] = kn
            v_ref[head_rows, :] = vb[:, sl]


def _abin(h, w_in, lb_param, g_q, g_k, layer, tile):
    n, d = h.shape
    assert d == N_HEADS * HEAD_DIM
    feat = [BF16, F32, BF16, BF16, BF16, BF16, BF16]
    out_shape = [jax.ShapeDtypeStruct((n, d), t) for t in feat]
    out_specs = [_rows(tile, d) for _ in feat]
    for _ in range(2):
        out_shape.append(jax.ShapeDtypeStruct((n * N_HEADS, HEAD_DIM), F32))
        out_specs.append(_rows(tile * N_HEADS, HEAD_DIM))
    return pl.pallas_call(
        functools.partial(_abin_body, layer),
        out_shape=out_shape, grid=(n // tile,),
        in_specs=[_rows(tile, d), w_in[1], _resident(lb_param.shape), g_q[1], g_k[1]],
        out_specs=out_specs,
        compiler_params=_params("parallel"), name="ab_in_proj")(
            h, w_in[0], lb_param, g_q[0], g_k[0])


def _hgrn_scores_any_decay(q, k, lf2, chunk):
    row = lax.broadcasted_iota(jnp.int32, (chunk, chunk), 0)
    col = lax.broadcasted_iota(jnp.int32, (chunk, chunk), 1)
    t1 = lax.broadcasted_iota(jnp.int32, (chunk, 1), 0)
    scores = jnp.where(row == col, jnp.sum(q * k, axis=-1, keepdims=True), 0.0)
    half = chunk // 2
    while half >= 1:
        blk = 2 * half
        same = (row // blk) == (col // blk)
        mid = (row // blk) * blk + half
        right = row >= mid
        seg = same & ((right & (col >= mid) & (col <= row)) | ((col > row) & (col < mid)))
        d2 = _dot(jnp.where(seg, 1.0, 0.0).astype(BF16), lf2)
        decay = jnp.exp(d2[:, :HEAD_DIM] + d2[:, HEAD_DIM:])
        right1 = (t1 % blk) >= half
        qe = jnp.where(right1, q * decay, 0.0).astype(BF16)
        ke = jnp.where(right1, 0.0, k * decay).astype(BF16)
        scores = scores + jnp.where(same, _dot_nt(qe, ke), 0.0)
        half //= 2
    return scores


def _hgrn_body(chunk, n_chunks, heads, has_s0, factored, *refs):
    q_ref, lf_ref, v_ref, gt_ref = refs[:4]
    pos = 4
    s0_ref = refs[pos] if has_s0 else None
    pos += int(has_s0)
    go_ref, o_ref, so_ref = refs[pos:pos + 3]

    row = lax.broadcasted_iota(jnp.int32, (chunk, chunk), 0)
    col = lax.broadcasted_iota(jnp.int32, (chunk, chunk), 1)
    causal = col <= row
    lower = jnp.where(causal, 1.0, 0.0).astype(BF16)
    mid = chunk // 2
    head_cols = [slice(j * HEAD_DIM, (j + 1) * HEAD_DIM) for j in range(heads)]

    def step(i, states):
        rows = pl.ds(pl.multiple_of(i * chunk, chunk), chunk)
        lfs = [lf_ref[rows, cl] for cl in head_cols]
        lf2s = [jnp.concatenate(_split_bf16(lf), axis=1) for lf in lfs]
        c2s = [_dot(lower, lf2) for lf2 in lf2s]
        qs = [q_ref[rows, cl].astype(F32) for cl in head_cols]
        ks = [1.0 - jnp.exp(lf) for lf in lfs]
        cs = [c2[:, :HEAD_DIM] + c2[:, HEAD_DIM:] for c2 in c2s]
        c_lasts = [c[chunk - 1:chunk, :] for c in cs]
        vs = [v_ref[rows, cl] for cl in head_cols]
        if factored:
            c_mids = [c[mid - 1:mid, :] for c in cs]
            rel = [c - cm for c, cm in zip(cs, c_mids)]
            qes = [(q * jnp.exp(a)).astype(BF16) for q, a in zip(qs, rel)]
            kes = [(k * jnp.exp(-a)).astype(BF16) for k, a in zip(ks, rel)]
            rhs = [jnp.concatenate([ke, (st * jnp.exp(cm)).astype(BF16)], axis=0)
                   for ke, st, cm in zip(kes, states, c_mids)]
            both = [_dot_nt(qe, r) for qe, r in zip(qes, rhs)]
            grown = [_dot_tn(v, ke) * jnp.exp(cl - cm) for v, ke, cl, cm in zip(vs, kes, c_lasts, c_mids)]
            scores = [jnp.where(causal, bt[:, :chunk], 0.0) for bt in both]
            inter = [bt[:, chunk:] for bt in both]
        else:
            inter = [_dot_nt((q * jnp.exp(c)).astype(BF16), st.astype(BF16)) for q, c, st in zip(qs, cs, states)]
            scores = [_hgrn_scores_any_decay(q, k, lf2, chunk) for q, k, lf2 in zip(qs, ks, lf2s)]
            kds = [(k * jnp.exp(cl - c)).astype(BF16) for k, cl, c in zip(ks, c_lasts, cs)]
            grown = [_dot_tn(v, kd) for v, kd in zip(vs, kds)]
        intra = [_dot(s.astype(BF16), v) for s, v in zip(scores, vs)]
        for j, cl in enumerate(head_cols):
            o = _rms_rows(inter[j] + intra[j], go_ref[j]) * gt_ref[rows, cl].astype(F32)
            o_ref[rows, cl] = o.astype(o_ref.dtype)
        return tuple(st * jnp.exp(cl) + g for st, cl, g in zip(states, c_lasts, grown))

    if has_s0:
        states = tuple(s0_ref[j].T for j in range(heads))
    else:
        states = tuple(jnp.zeros((HEAD_DIM, HEAD_DIM), F32) for _ in range(heads))
    states = lax.fori_loop(0, n_chunks, step, states, unroll=8) if n_chunks > 1 else step(0, states)
    for j in range(heads):
        so_ref[j] = states[j].T


def _hgrn_call(tokens, s0, g_out, batch, seq, chunk, heads, factored):
    n, d = tokens[0].shape
    has_s0 = s0 is not None
    width = heads * HEAD_DIM
    tok = pl.BlockSpec((seq, width), lambda b, h: (b, h))
    state = pl.BlockSpec((None, heads, HEAD_DIM, HEAD_DIM), lambda b, h: (b, h, 0, 0))
    in_specs = [tok] * len(tokens)
    args = list(tokens)
    if has_s0:
        in_specs.append(state)
        args.append(s0)
    in_specs.append(pl.BlockSpec((heads, 1, HEAD_DIM), lambda b, h: (h, 0, 0)))
    args.append(g_out.reshape(N_HEADS, 1, HEAD_DIM))
    return pl.pallas_call(
        functools.partial(_hgrn_body, chunk, seq // chunk, heads, has_s0, factored),
        out_shape=[jax.ShapeDtypeStruct((n, d), BF16),
                   jax.ShapeDtypeStruct((batch, N_HEADS, HEAD_DIM, HEAD_DIM), F32)],
        grid=(batch, N_HEADS // heads), in_specs=in_specs, out_specs=[tok, state],
        compiler_params=_params("parallel", "parallel"), name="hgrn2")(*args)


def _hgrn(tokens, s0, g_out, lb_param, layer, batch, seq):
    chunk = min(seq, HGRN_CHUNK)
    heads = N_HEADS if seq < HGRN_CHUNK else HGRN_HEADS_PER_STEP
    call = functools.partial(_hgrn_call, tokens, s0, g_out, batch, seq, chunk, heads)
    lb = jnp.cumsum(jax.nn.softmax(lb_param.astype(F32), axis=0), axis=0)[layer]
    reach = (chunk // 2) * jnp.max(-jnp.log(lb))
    return lax.cond(reach <= HGRN_SAFE_EXPONENT, lambda: call(True), lambda: call(False))


def _sb_logits_many(items):
    zs = [_dot_nt(q, kb) for q, kb, _, _ in items]
    sps = [jnp.maximum(z, 0.0) + jnp.log(1.0 + jnp.exp(-jnp.abs(z))) for z in zs]
    costs = [sp if it[3] is None else jnp.where(it[3], sp, 0.0) for sp, it in zip(sps, items)]
    suffix = [_dot(c.astype(BF16), it[2]) for c, it in zip(costs, items)]
    return [(z - sp - sf, jnp.sum(c, axis=-1, keepdims=True)) for z, sp, sf, c in zip(zs, sps, suffix, costs)]


def _sb_accumulate(logits, cost, vb, mask, acc, spent):
    w = jnp.exp(logits - spent)
    if mask is not None:
        w = jnp.where(mask, w, 0.0)
    return acc + _dot(w.astype(BF16), vb), spent + cost


def _strict_upper(n):
    j = lax.broadcasted_iota(jnp.int32, (n, n), 0)
    k = lax.broadcasted_iota(jnp.int32, (n, n), 1)
    return jnp.where(j > k, 1.0, 0.0).astype(BF16)


def _sb_body(blk, n_q, heads, past_blk, n_past, *refs):
    q_ref, k_ref, v_ref = refs[:3]
    if n_past:
        kp_new_ref, vp_new_ref, kp_hbm, vp_hbm, o_ref, kbuf, vbuf, sem = refs[3:]
    else:
        o_ref = refs[3]

    t = lax.broadcasted_iota(jnp.int32, (blk, blk), 0)
    s = lax.broadcasted_iota(jnp.int32, (blk, blk), 1)
    diag_mask = s < t
    upper = _strict_upper(blk)
    upper_past = _strict_upper(past_blk) if n_past else None
    head_cols = [slice(j * HEAD_DIM, (j + 1) * HEAD_DIM) for j in range(heads)]

    def still_live(spent):
        return functools.reduce(jnp.minimum, [jnp.min(r) for r in spent]) < -SB_EXIT

    for qi in range(n_q):
        rows = pl.ds(qi * blk, blk)
        qs = [q_ref[rows, c] for c in head_cols]
        n_older = qi + n_past
        items = [(qs[j], k_ref[rows, c], upper, diag_mask) for j, c in enumerate(head_cols)]
        values = [(v_ref[rows, c], diag_mask) for c in head_cols]
        if qi > 0:
            prev = pl.ds((qi - 1) * blk, blk)
            items += [(qs[j], k_ref[prev, c], upper, None) for j, c in enumerate(head_cols)]
            values += [(v_ref[prev, c], None) for c in head_cols]
        elif n_past:
            items += [(qs[j], kp_new_ref[_head_rows(j, past_blk), :].astype(BF16), upper_past, None)
                      for j in range(heads)]
            values += [(vp_new_ref[_head_rows(j, past_blk), :].astype(BF16), None) for j in range(heads)]
        logits = _sb_logits_many(items)
        accs = [jnp.zeros((blk, HEAD_DIM), F32) for _ in range(heads)]
        runs = [jnp.zeros((blk, 1), F32) for _ in range(heads)]
        for n, ((lg, stay), (vb, mask)) in enumerate(zip(logits, values)):
            j = n % heads
            accs[j], runs[j] = _sb_accumulate(lg, stay, vb, mask, accs[j], runs[j])

        if n_older > 1:
            def cond(carry):
                return (carry[0] < n_older) & carry[1]

            def body(carry, qi=qi, qs=qs):
                idx = carry[0]
                accs, runs = list(carry[2]), list(carry[3])
                if n_past:
                    src = pl.ds(pl.multiple_of((n_past - 1 - idx) * past_blk * N_HEADS, past_blk * N_HEADS),
                                past_blk * N_HEADS)
                    b = pl.program_id(0)
                    copies = [pltpu.make_async_copy(kp_hbm.at[b, src, :], kbuf, sem.at[0]),
                              pltpu.make_async_copy(vp_hbm.at[b, src, :], vbuf, sem.at[1])]
                    for cp in copies:
                        cp.start()
                    for cp in copies:
                        cp.wait()
                if n_past:
                    items = [(qs[j], kbuf[_head_rows(j, past_blk), :].astype(BF16), upper_past, None)
                             for j in range(heads)]
                    vbs = [vbuf[_head_rows(j, past_blk), :].astype(BF16) for j in range(heads)]
                else:
                    kr = pl.ds(pl.multiple_of((qi - 1 - idx) * blk, blk), blk)
                    items = [(qs[j], k_ref[kr, c], upper, None) for j, c in enumerate(head_cols)]
                    vbs = [v_ref[kr, c] for c in head_cols]
                for j, (lg, stay) in enumerate(_sb_logits_many(items)):
                    accs[j], runs[j] = _sb_accumulate(lg, stay, vbs[j], None, accs[j], runs[j])
                return idx + 1, still_live(runs), tuple(accs), tuple(runs)

            _, _, accs, _ = lax.while_loop(cond, body, (jnp.int32(1), still_live(runs), tuple(accs), tuple(runs)))
        for j, c in enumerate(head_cols):
            o_ref[rows, c] = accs[j].astype(o_ref.dtype)


def _stick_breaking(qn, kb, vb, k_past, v_past, batch, seq):
    n, d = qn.shape
    blk = min(seq, SB_BLOCK)
    n_q = seq // blk
    heads = N_HEADS if seq < SB_BLOCK else SB_HEADS_PER_STEP
    tok = pl.BlockSpec((seq, heads * HEAD_DIM), lambda b, h: (b, h))
    in_specs = [tok, tok, tok]
    args = [qn, kb, vb]
    scratch = []
    past_blk = n_past = 0
    if k_past is not None:
        assert n_q == 1 and heads == N_HEADS, "cached keys are only supported for a single query block"
        past_len = k_past.shape[1]
        past_blk = min(past_len, SB_BLOCK)
        n_past = past_len // past_blk
        slab = past_blk * N_HEADS
        newest = pl.BlockSpec((None, slab, HEAD_DIM), lambda b, h: (b, n_past - 1, 0))
        hbm = pl.BlockSpec(memory_space=pl.ANY)
        in_specs += [newest, newest, hbm, hbm]
        kp = k_past.reshape(batch, past_len * N_HEADS, HEAD_DIM)
        vp = v_past.reshape(batch, past_len * N_HEADS, HEAD_DIM)
        args += [kp, vp, kp, vp]
        scratch = [pltpu.VMEM((slab, HEAD_DIM), F32), pltpu.VMEM((slab, HEAD_DIM), F32),
                   pltpu.SemaphoreType.DMA((2,))]
    return pl.pallas_call(
        functools.partial(_sb_body, blk, n_q, heads, past_blk, n_past),
        out_shape=jax.ShapeDtypeStruct((n, d), BF16),
        grid=(batch, N_HEADS // heads), in_specs=in_specs, out_specs=tok,
        scratch_shapes=scratch,
        compiler_params=_params("parallel", "parallel"), name="stick_breaking")(*args)


def _twice_gelu_tanh(x):
    t = jnp.tanh(x * (0.7978845608028654 + (0.7978845608028654 * 0.044715) * (x * x)))
    return x + x * t


def _gmlp_body(length, n_sub, emit_v, h_ref, w_ref, g_ref, b_ref, ws_ref, bs_ref, y_ref, *v_out):
    h = h_ref[...]
    d_c = w_ref.shape[1] // 2
    dg = d_c // GMLP_GROUPS
    i = lax.broadcasted_iota(jnp.int32, (length, length), 0)
    j = lax.broadcasted_iota(jnp.int32, (length, length), 1)
    mask = (j // CHUNK) <= (i // CHUNK)
    half_bias = 0.5 * bs_ref[...]

    vs = [_twice_gelu_tanh(_dot(h, w_ref[:, d_c + g * dg:d_c + (g + 1) * dg])) for g in range(GMLP_GROUPS)]
    total = functools.reduce(jnp.add, [jnp.sum(v, axis=-1, keepdims=True) for v in vs])
    total_sq = functools.reduce(jnp.add, [jnp.sum(v * v, axis=-1, keepdims=True) for v in vs])
    mean = total / d_c
    inv = lax.rsqrt(total_sq / d_c - mean * mean + 4.0 * EPS)
    for g in range(GMLP_GROUPS):
        cols = slice(g * dg, (g + 1) * dg)
        vn = (vs[g] - mean) * inv * g_ref[:, cols] + b_ref[:, cols]
        if emit_v:
            v_out[0][:, cols] = vn
        vn = vn.astype(BF16)
        u = _twice_gelu_tanh(_dot(h, w_ref[:, cols]))
        w = jnp.where(mask, 0.5 * ws_ref[g], 0.0).astype(BF16)
        for c in range(n_sub):
            rows = slice(c * length, (c + 1) * length)
            mixed = _dot(w, vn[rows]) + half_bias[:, g:g + 1]
            y_ref[rows, cols] = (u[rows] * mixed).astype(BF16)


def _gmlp(h, w_in, ln_g, ln_b, w_s, b_s, length, tile, emit_v):
    n, d = h.shape
    d_c = w_in[0].shape[-1] // 2
    bs = b_s.T
    out_shape = [jax.ShapeDtypeStruct((n, d_c), BF16)]
    out_specs = [_rows(tile, d_c)]
    if emit_v:
        out_shape.append(jax.ShapeDtypeStruct((n, d_c), F32))
        out_specs.append(_rows(tile, d_c))
    outs = pl.pallas_call(
        functools.partial(_gmlp_body, length, tile // length, emit_v),
        out_shape=out_shape, grid=(n // tile,),
        in_specs=[_rows(tile, d), w_in[1], ln_g[1], ln_b[1], _resident(w_s.shape), _resident(bs.shape)],
        out_specs=out_specs,
        compiler_params=_params("parallel"), name="gmlp")(h, w_in[0], ln_g[0], ln_b[0], w_s, bs)
    return (outs[0], outs[1]) if emit_v else (outs[0], None)


def _trunk(x3, p, hgrn_state, sb_k, sb_v, want_cv):
    batch, seq, d = x3.shape
    n = batch * seq
    tile = min(TOKEN_TILE, n)
    depth = p['ffn1_norm'].shape[0]
    x = x3.reshape(n, d)
    new_k, new_v, new_s, new_cv = [], [], [], []
    pre = []

    def row(name, l):
        v = p[name]
        return _layer(v.reshape(v.shape[0], 1, -1), l)

    for l in range(depth):
        x, h = _ffn(x, pre, row('ffn1_norm', l), _layer(p['ffn1_w_gate'], l), _layer(p['ffn1_w_up'], l),
                    _layer(p['ffn1_w_down'], l), row('mix_norm', l), tile)
        if l % 2 == 0:
            e = l // 2
            *hgrn_in, qb, kb, vb, k, v = _abin(h, _layer(p['ab_w_in'], e), p['ab_lb'], row('ab_g_q', e),
                                                     row('ab_g_k', e), e, tile)
            mix_a, s_new = _hgrn(hgrn_in, None if hgrn_state is None else hgrn_state[e],
                                 p['ab_g_out'][e], p['ab_lb'], e, batch, seq)
            mix_b = _stick_breaking(qb, kb, vb, None if sb_k is None else sb_k[e],
                                    None if sb_v is None else sb_v[e], batch, seq)
            pre = [(mix_a, _layer(p['ab_w_out'], e, 0, 2)), (mix_b, _layer(p['ab_w_out'], e, 1, 2))]
            new_k.append(k.reshape(batch, seq, N_HEADS, HEAD_DIM))
            new_v.append(v.reshape(batch, seq, N_HEADS, HEAD_DIM))
            new_s.append(s_new)
        else:
            o = l // 2
            length = min(seq, GMLP_LEN)
            y, cv = _gmlp(h, _layer(p['c_w_in'], o), row('c_ln_g', o), row('c_ln_b', o),
                          p['c_w_s'][o, :, :length, :length], p['c_b_s'][o, :, :length], length, tile, want_cv)
            pre = [(y, _layer(p['c_w_out'], o))]
            if want_cv:
                new_cv.append(cv.reshape(batch, seq, -1))
        x, _ = _ffn(x, pre, row('ffn2_norm', l), _layer(p['ffn2_w_gate'], l), _layer(p['ffn2_w_up'], l),
                    _layer(p['ffn2_w_down'], l), None, tile)
        pre = []
    return x.reshape(batch, seq, d), new_k, new_v, new_s, new_cv


_MATMUL_WEIGHTS = ('ffn1_w_gate', 'ffn1_w_up', 'ffn1_w_down', 'ffn2_w_gate', 'ffn2_w_up',
                   'ffn2_w_down', 'ab_w_in', 'ab_w_out', 'c_w_in', 'c_w_out')


def kernel(x_prompt, x_sample, cache_sb_k, cache_sb_v, state_hgrn, ffn1_norm, ffn1_w_gate, ffn1_w_up, ffn1_w_down, mix_norm, ffn2_norm, ffn2_w_gate, ffn2_w_up, ffn2_w_down, ab_w_in, ab_lb, ab_g_out, ab_g_q, ab_g_k, ab_w_out, c_w_in, c_ln_g, c_ln_b, c_w_s, c_b_s, c_w_out):
    p = {'ffn1_norm': ffn1_norm, 'ffn1_w_gate': ffn1_w_gate, 'ffn1_w_up': ffn1_w_up, 'ffn1_w_down': ffn1_w_down,
         'mix_norm': mix_norm, 'ffn2_norm': ffn2_norm, 'ffn2_w_gate': ffn2_w_gate, 'ffn2_w_up': ffn2_w_up,
         'ffn2_w_down': ffn2_w_down, 'ab_w_in': ab_w_in, 'ab_lb': ab_lb, 'ab_g_out': ab_g_out, 'ab_g_q': ab_g_q,
         'ab_g_k': ab_g_k, 'ab_w_out': ab_w_out, 'c_w_in': c_w_in, 'c_ln_g': c_ln_g, 'c_ln_b': c_ln_b,
         'c_w_s': c_w_s, 'c_b_s': c_b_s, 'c_w_out': c_w_out}
    for name in _MATMUL_WEIGHTS:
        p[name] = p[name].astype(BF16)
    y_prompt, pk, pv, ps, _ = _trunk(x_prompt, p, None, None, None, False)
    y_sample, sk, sv, ss, scv = _trunk(x_sample, p, state_hgrn, cache_sb_k, cache_sb_v, True)
    return (y_prompt, y_sample, jnp.stack(pk), jnp.stack(pv), jnp.stack(ps),
            jnp.stack(sk), jnp.stack(sv), jnp.stack(ss), jnp.stack(scv))
```

```python
import functools

import jax
import jax.numpy as jnp
from jax import lax
from jax.experimental import pallas as pl
from jax.experimental.pallas import tpu as pltpu

F32 = jnp.float32
BF16 = jnp.bfloat16

EPS = 1e-6
FFN_SCALE = 0.5
HEAD_DIM = 128
N_HEADS = 8
CHUNK = 64
GMLP_LEN = 128
GMLP_GROUPS = 8
V7X_VMEM_LIMIT_BYTES = 56 * 1024 * 1024
TOKEN_TILE = 512
FFN_ROW_GROUPS = 2
AB_ROW_GROUPS = 2
HGRN_CHUNK = 128
HGRN_HEADS_PER_STEP = 4
HGRN_SAFE_EXPONENT = 60.0
SB_BLOCK = 256
SB_HEADS_PER_STEP = 4
SB_EXIT = -105.0


def _params(*sem):
    return pltpu.CompilerParams(dimension_semantics=sem, vmem_limit_bytes=V7X_VMEM_LIMIT_BYTES)


def _resident(shape):
    zeros = (0,) * len(shape)
    return pl.BlockSpec(shape, lambda *_: zeros, pipeline_mode=pl.Buffered(1))


def _layer(stacked, layer, row_block=0, n_row_blocks=1):
    rows = stacked.shape[1] // n_row_blocks
    block = (None, rows) + stacked.shape[2:]
    index = (layer, row_block) + (0,) * (stacked.ndim - 2)
    return stacked, pl.BlockSpec(block, lambda *_: index, pipeline_mode=pl.Buffered(1))


def _rows(tile, width, first_tile=0):
    return pl.BlockSpec((tile, width), lambda i: (i + first_tile, 0))


def _whole(arr, tile):
    return [(arr, 0, arr.shape[0] // tile)]


def _segment_index(start, first_tile, n_tiles, i):
    return jnp.clip(i - start, 0, n_tiles - 1) + first_tile, 0


def _segment_specs(segments, tile, total_tiles):
    specs, start = [], 0
    for arr, first_tile, n_tiles in segments:
        specs.append(pl.BlockSpec((tile, arr.shape[1]),
                                  functools.partial(_segment_index, start, first_tile, n_tiles)))
        start += n_tiles
    assert start == total_tiles and len(segments) <= 2
    return specs


def _segment_rows(refs, n_first, rows):
    if len(refs) == 1:
        return refs[0][rows, :]
    return jnp.where(pl.program_id(0) < n_first, refs[0][rows, :], refs[1][rows, :])


def _sigmoid(x):
    return 1.0 / (1.0 + jnp.exp(-x))


def _rms_rows(x, g):
    ms = jnp.mean(x * x, axis=-1, keepdims=True)
    return x * lax.rsqrt(ms + EPS) * g


def _dot(a, b):
    return jnp.dot(a, b, preferred_element_type=F32)


def _dot_nt(a, b):
    return lax.dot_general(a, b, (((1,), (1,)), ((), ())), preferred_element_type=F32)


def _dot_tn(a, b):
    return lax.dot_general(a, b, (((0,), (0,)), ((), ())), preferred_element_type=F32)


def _split_bf16(x):
    hi = x.astype(BF16)
    lo = (x - hi.astype(F32)).astype(BF16)
    return hi, lo


def _head_rows(head, n):
    return pl.ds(head, n, stride=N_HEADS)


def _ffn_body(seg_counts, n_first, has_next, *refs):
    pos = seg_counts[0]
    x_refs = refs[:pos]
    pre = []
    for count in seg_counts[1:]:
        pre.append((refs[pos:pos + count], refs[pos + count]))
        pos += count + 1
    g_ref, wg_ref, wu_ref, wd_ref = refs[pos:pos + 4]
    pos += 4
    g2_ref = refs[pos] if has_next else None
    pos += int(has_next)
    xo_ref = refs[pos]
    ho_ref = refs[pos + 1] if has_next else None

    tile = xo_ref.shape[0]
    parts = [slice(k * (tile // FFN_ROW_GROUPS), (k + 1) * (tile // FFN_ROW_GROUPS)) for k in range(FFN_ROW_GROUPS)]
    xs = [_segment_rows(x_refs, n_first, r) for r in parts]
    for m_refs, w_ref in pre:
        xs = [x + _dot(_segment_rows(m_refs, n_first, r), w_ref[...]) for x, r in zip(xs, parts)]
    hs = [_rms_rows(x, g_ref[...]).astype(BF16) for x in xs]
    gates = [_dot(h, wg_ref[...]) for h in hs]
    ups = [_dot(h, wu_ref[...]) for h in hs]
    acts = [(a * _sigmoid(a) * b).astype(BF16) for a, b in zip(gates, ups)]
    downs = [_dot(act, wd_ref[...]) for act in acts]
    for x, y, r in zip(xs, downs, parts):
        x = x + FFN_SCALE * y
        xo_ref[r, :] = x
        if has_next:
            ho_ref[r, :] = _rms_rows(x, g2_ref[...]).astype(BF16)


def _ffn(x, pre, g, wg, wu, wd, g_next, tile):
    n_tiles = sum(seg[2] for seg in x)
    d = x[0][0].shape[1]
    has_next = g_next is not None
    operands = [x] + [m for m, _ in pre]
    n_first = max(segs[0][2] if len(segs) == 2 else 0 for segs in operands)
    assert all(len(segs) == 1 or segs[0][2] == n_first for segs in operands)
    in_specs = _segment_specs(x, tile, n_tiles)
    args = [seg[0] for seg in x]
    for m, (w, w_spec) in pre:
        in_specs += _segment_specs(m, tile, n_tiles) + [w_spec]
        args += [seg[0] for seg in m] + [w]
    for arr, spec in (g, wg, wu, wd):
        in_specs.append(spec)
        args.append(arr)
    out_shape = [jax.ShapeDtypeStruct((n_tiles * tile, d), F32)]
    out_specs = [_rows(tile, d)]
    if has_next:
        in_specs.append(g_next[1])
        args.append(g_next[0])
        out_shape.append(jax.ShapeDtypeStruct((n_tiles * tile, d), BF16))
        out_specs.append(_rows(tile, d))
    outs = pl.pallas_call(
        functools.partial(_ffn_body, tuple(len(segs) for segs in operands), n_first, has_next),
        out_shape=out_shape, grid=(n_tiles,), in_specs=in_specs, out_specs=out_specs,
        compiler_params=_params("parallel"), name="ffn")(*args)
    return (outs[0], outs[1]) if has_next else (outs[0], None)


def _abin_body(layer, h_ref, w_ref, lbp_ref, gq_ref, gk_ref,
               qa_ref, lf_ref, va_ref, ga_ref, qb_ref, kb_ref, vb_ref, k_ref, v_ref):
    tile, d = h_ref.shape
    group = tile // AB_ROW_GROUPS

    lbp = lbp_ref[...]
    e = jnp.exp(lbp - jnp.max(lbp, axis=0, keepdims=True))
    lb = jnp.sum(e[:layer + 1], axis=0, keepdims=True) / jnp.sum(e, axis=0, keepdims=True)

    for k in range(AB_ROW_GROUPS):
        r = slice(k * group, (k + 1) * group)
        h = h_ref[r, :]

        def seg(i, h=h):
            return _dot(h, w_ref[:, i * d:(i + 1) * d])

        qa = seg(0)
        qa_ref[r, :] = (qa * _sigmoid(qa)).astype(BF16)
        f = lb + (1.0 - lb) * _sigmoid(seg(1))
        lf_ref[r, :] = jnp.log(f)
        va_ref[r, :] = seg(2).astype(BF16)
        ga_ref[r, :] = _sigmoid(seg(3)).astype(BF16)

        qb = seg(4)
        kb = seg(5)
        vb = seg(6)
        vb_ref[r, :] = vb.astype(BF16)
        for hh in range(N_HEADS):
            sl = slice(hh * HEAD_DIM, (hh + 1) * HEAD_DIM)
            qb_ref[r, sl] = (_rms_rows(qb[:, sl], gq_ref[...]) * HEAD_DIM ** -0.5).astype(BF16)
            kn = _rms_rows(kb[:, sl], gk_ref[...])
            kb_ref[r, sl] = kn.astype(BF16)
            head_rows = pl.ds(k * group * N_HEADS + hh, group, stride=N_HEADS)
            k_ref[head_rows, :] = kn
            v_ref[head_rows, :] = vb[:, sl]


def _abin(h, first_tile, n, w_in, lb_param, g_q, g_k, layer, tile):
    d = h.shape[1]
    assert d == N_HEADS * HEAD_DIM
    feat = [BF16, F32, BF16, BF16, BF16, BF16, BF16]
    out_shape = [jax.ShapeDtypeStruct((n, d), t) for t in feat]
    out_specs = [_rows(tile, d) for _ in feat]
    for _ in range(2):
        out_shape.append(jax.ShapeDtypeStruct((n * N_HEADS, HEAD_DIM), F32))
        out_specs.append(_rows(tile * N_HEADS, HEAD_DIM))
    return pl.pallas_call(
        functools.partial(_abin_body, layer),
        out_shape=out_shape, grid=(n // tile,),
        in_specs=[_rows(tile, d, first_tile), w_in[1], _resident(lb_param.shape), g_q[1], g_k[1]],
        out_specs=out_specs,
        compiler_params=_params("parallel"), name="ab_in_proj")(
            h, w_in[0], lb_param, g_q[0], g_k[0])


def _hgrn_scores_any_decay(q, k, lf2, chunk):
    row = lax.broadcasted_iota(jnp.int32, (chunk, chunk), 0)
    col = lax.broadcasted_iota(jnp.int32, (chunk, chunk), 1)
    t1 = lax.broadcasted_iota(jnp.int32, (chunk, 1), 0)
    scores = jnp.where(row == col, jnp.sum(q * k, axis=-1, keepdims=True), 0.0)
    half = chunk // 2
    while half >= 1:
        blk = 2 * half
        same = (row // blk) == (col // blk)
        mid = (row // blk) * blk + half
        right = row >= mid
        seg = same & ((right & (col >= mid) & (col <= row)) | ((col > row) & (col < mid)))
        d2 = _dot(jnp.where(seg, 1.0, 0.0).astype(BF16), lf2)
        decay = jnp.exp(d2[:, :HEAD_DIM] + d2[:, HEAD_DIM:])
        right1 = (t1 % blk) >= half
        qe = jnp.where(right1, q * decay, 0.0).astype(BF16)
        ke = jnp.where(right1, 0.0, k * decay).astype(BF16)
        scores = scores + jnp.where(same, _dot_nt(qe, ke), 0.0)
        half //= 2
    return scores


def _hgrn_body(chunk, n_chunks, heads, has_s0, factored, *refs):
    q_ref, lf_ref, v_ref, gt_ref = refs[:4]
    pos = 4
    s0_ref = refs[pos] if has_s0 else None
    pos += int(has_s0)
    go_ref, o_ref, so_ref = refs[pos:pos + 3]

    row = lax.broadcasted_iota(jnp.int32, (chunk, chunk), 0)
    col = lax.broadcasted_iota(jnp.int32, (chunk, chunk), 1)
    causal = col <= row
    lower = jnp.where(causal, 1.0, 0.0).astype(BF16)
    mid = chunk // 2
    head_cols = [slice(j * HEAD_DIM, (j + 1) * HEAD_DIM) for j in range(heads)]

    def step(i, states):
        rows = pl.ds(pl.multiple_of(i * chunk, chunk), chunk)
        lfs = [lf_ref[rows, cl] for cl in head_cols]
        lf2s = [jnp.concatenate(_split_bf16(lf), axis=1) for lf in lfs]
        c2s = [_dot(lower, lf2) for lf2 in lf2s]
        qs = [q_ref[rows, cl].astype(F32) for cl in head_cols]
        ks = [1.0 - jnp.exp(lf) for lf in lfs]
        cs = [c2[:, :HEAD_DIM] + c2[:, HEAD_DIM:] for c2 in c2s]
        c_lasts = [c[chunk - 1:chunk, :] for c in cs]
        vs = [v_ref[rows, cl] for cl in head_cols]
        if factored:
            c_mids = [c[mid - 1:mid, :] for c in cs]
            rel = [c - cm for c, cm in zip(cs, c_mids)]
            qes = [(q * jnp.exp(a)).astype(BF16) for q, a in zip(qs, rel)]
            kes = [(k * jnp.exp(-a)).astype(BF16) for k, a in zip(ks, rel)]
            rhs = [jnp.concatenate([ke, (st * jnp.exp(cm)).astype(BF16)], axis=0)
                   for ke, st, cm in zip(kes, states, c_mids)]
            both = [_dot_nt(qe, r) for qe, r in zip(qes, rhs)]
            grown = [_dot_tn(v, ke) * jnp.exp(cl - cm) for v, ke, cl, cm in zip(vs, kes, c_lasts, c_mids)]
            scores = [jnp.where(causal, bt[:, :chunk], 0.0) for bt in both]
            inter = [bt[:, chunk:] for bt in both]
        else:
            inter = [_dot_nt((q * jnp.exp(c)).astype(BF16), st.astype(BF16)) for q, c, st in zip(qs, cs, states)]
            scores = [_hgrn_scores_any_decay(q, k, lf2, chunk) for q, k, lf2 in zip(qs, ks, lf2s)]
            kds = [(k * jnp.exp(cl - c)).astype(BF16) for k, cl, c in zip(ks, c_lasts, cs)]
            grown = [_dot_tn(v, kd) for v, kd in zip(vs, kds)]
        intra = [_dot(s.astype(BF16), v) for s, v in zip(scores, vs)]
        for j, cl in enumerate(head_cols):
            o = _rms_rows(inter[j] + intra[j], go_ref[j]) * gt_ref[rows, cl].astype(F32)
            o_ref[rows, cl] = o.astype(o_ref.dtype)
        return tuple(st * jnp.exp(cl) + g for st, cl, g in zip(states, c_lasts, grown))

    if has_s0:
        states = tuple(s0_ref[j].T for j in range(heads))
    else:
        states = tuple(jnp.zeros((HEAD_DIM, HEAD_DIM), F32) for _ in range(heads))
    states = lax.fori_loop(0, n_chunks, step, states, unroll=8) if n_chunks > 1 else step(0, states)
    for j in range(heads):
        so_ref[j] = states[j].T


def _hgrn_call(tokens, s0, g_out, batch, seq, chunk, heads, factored):
    n, d = tokens[0].shape
    has_s0 = s0 is not None
    width = heads * HEAD_DIM
    tok = pl.BlockSpec((seq, width), lambda b, h: (b, h))
    state = pl.BlockSpec((None, heads, HEAD_DIM, HEAD_DIM), lambda b, h: (b, h, 0, 0))
    in_specs = [tok] * len(tokens)
    args = list(tokens)
    if has_s0:
        in_specs.append(state)
        args.append(s0)
    in_specs.append(pl.BlockSpec((heads, 1, HEAD_DIM), lambda b, h: (h, 0, 0)))
    args.append(g_out.reshape(N_HEADS, 1, HEAD_DIM))
    return pl.pallas_call(
        functools.partial(_hgrn_body, chunk, seq // chunk, heads, has_s0, factored),
        out_shape=[jax.ShapeDtypeStruct((n, d), BF16),
                   jax.ShapeDtypeStruct((batch, N_HEADS, HEAD_DIM, HEAD_DIM), F32)],
        grid=(batch, N_HEADS // heads), in_specs=in_specs, out_specs=[tok, state],
        compiler_params=_params("parallel", "parallel"), name="hgrn2")(*args)


def _hgrn(tokens, s0, g_out, lb_param, layer, batch, seq):
    chunk = min(seq, HGRN_CHUNK)
    heads = N_HEADS if seq < HGRN_CHUNK else HGRN_HEADS_PER_STEP
    call = functools.partial(_hgrn_call, tokens, s0, g_out, batch, seq, chunk, heads)
    lb = jnp.cumsum(jax.nn.softmax(lb_param.astype(F32), axis=0), axis=0)[layer]
    reach = (chunk // 2) * jnp.max(-jnp.log(lb))
    return lax.cond(reach <= HGRN_SAFE_EXPONENT, lambda: call(True), lambda: call(False))


def _sb_logits_many(items):
    zs = [_dot_nt(q, kb) for q, kb, _, _ in items]
    sps = [jnp.maximum(z, 0.0) + jnp.log(1.0 + jnp.exp(-jnp.abs(z))) for z in zs]
    costs = [sp if it[3] is None else jnp.where(it[3], sp, 0.0) for sp, it in zip(sps, items)]
    suffix = [_dot(c.astype(BF16), it[2]) for c, it in zip(costs, items)]
    return [(z - sp - sf, jnp.sum(c, axis=-1, keepdims=True)) for z, sp, sf, c in zip(zs, sps, suffix, costs)]


def _sb_accumulate(logits, cost, vb, mask, acc, spent):
    w = jnp.exp(logits - spent)
    if mask is not None:
        w = jnp.where(mask, w, 0.0)
    return acc + _dot(w.astype(BF16), vb), spent + cost


def _strict_upper(n):
    j = lax.broadcasted_iota(jnp.int32, (n, n), 0)
    k = lax.broadcasted_iota(jnp.int32, (n, n), 1)
    return jnp.where(j > k, 1.0, 0.0).astype(BF16)


def _sb_body(blk, n_q, heads, past_blk, n_past, *refs):
    q_ref, k_ref, v_ref = refs[:3]
    if n_past:
        kp_new_ref, vp_new_ref, kp_hbm, vp_hbm, o_ref, kbuf, vbuf, sem = refs[3:]
    else:
        o_ref = refs[3]

    t = lax.broadcasted_iota(jnp.int32, (blk, blk), 0)
    s = lax.broadcasted_iota(jnp.int32, (blk, blk), 1)
    diag_mask = s < t
    upper = _strict_upper(blk)
    upper_past = _strict_upper(past_blk) if n_past else None
    head_cols = [slice(j * HEAD_DIM, (j + 1) * HEAD_DIM) for j in range(heads)]

    def still_live(spent):
        return functools.reduce(jnp.minimum, [jnp.min(r) for r in spent]) < -SB_EXIT

    for qi in range(n_q):
        rows = pl.ds(qi * blk, blk)
        qs = [q_ref[rows, c] for c in head_cols]
        n_older = qi + n_past
        items = [(qs[j], k_ref[rows, c], upper, diag_mask) for j, c in enumerate(head_cols)]
        values = [(v_ref[rows, c], diag_mask) for c in head_cols]
        if qi > 0:
            prev = pl.ds((qi - 1) * blk, blk)
            items += [(qs[j], k_ref[prev, c], upper, None) for j, c in enumerate(head_cols)]
            values += [(v_ref[prev, c], None) for c in head_cols]
        elif n_past:
            items += [(qs[j], kp_new_ref[_head_rows(j, past_blk), :].astype(BF16), upper_past, None)
                      for j in range(heads)]
            values += [(vp_new_ref[_head_rows(j, past_blk), :].astype(BF16), None) for j in range(heads)]
        logits = _sb_logits_many(items)
        accs = [jnp.zeros((blk, HEAD_DIM), F32) for _ in range(heads)]
        runs = [jnp.zeros((blk, 1), F32) for _ in range(heads)]
        for n, ((lg, stay), (vb, mask)) in enumerate(zip(logits, values)):
            j = n % heads
            accs[j], runs[j] = _sb_accumulate(lg, stay, vb, mask, accs[j], runs[j])

        if n_older > 1:
            def cond(carry):
                return (carry[0] < n_older) & carry[1]

            def body(carry, qi=qi, qs=qs):
                idx = carry[0]
                accs, runs = list(carry[2]), list(carry[3])
                if n_past:
                    src = pl.ds(pl.multiple_of((n_past - 1 - idx) * past_blk * N_HEADS, past_blk * N_HEADS),
                                past_blk * N_HEADS)
                    b = pl.program_id(0)
                    copies = [pltpu.make_async_copy(kp_hbm.at[b, src, :], kbuf, sem.at[0]),
                              pltpu.make_async_copy(vp_hbm.at[b, src, :], vbuf, sem.at[1])]
                    for cp in copies:
                        cp.start()
                    for cp in copies:
                        cp.wait()
                if n_past:
                    items = [(qs[j], kbuf[_head_rows(j, past_blk), :].astype(BF16), upper_past, None)
                             for j in range(heads)]
                    vbs = [vbuf[_head_rows(j, past_blk), :].astype(BF16) for j in range(heads)]
                else:
                    kr = pl.ds(pl.multiple_of((qi - 1 - idx) * blk, blk), blk)
                    items = [(qs[j], k_ref[kr, c], upper, None) for j, c in enumerate(head_cols)]
                    vbs = [v_ref[kr, c] for c in head_cols]
                for j, (lg, stay) in enumerate(_sb_logits_many(items)):
                    accs[j], runs[j] = _sb_accumulate(lg, stay, vbs[j], None, accs[j], runs[j])
                return idx + 1, still_live(runs), tuple(accs), tuple(runs)

            _, _, accs, _ = lax.while_loop(cond, body, (jnp.int32(1), still_live(runs), tuple(accs), tuple(runs)))
        for j, c in enumerate(head_cols):
            o_ref[rows, c] = accs[j].astype(o_ref.dtype)


def _stick_breaking(qn, kb, vb, k_past, v_past, batch, seq):
    n, d = qn.shape
    blk = min(seq, SB_BLOCK)
    n_q = seq // blk
    heads = N_HEADS if seq < SB_BLOCK else SB_HEADS_PER_STEP
    tok = pl.BlockSpec((seq, heads * HEAD_DIM), lambda b, h: (b, h))
    in_specs = [tok, tok, tok]
    args = [qn, kb, vb]
    scratch = []
    past_blk = n_past = 0
    if k_past is not None:
        assert n_q == 1 and heads == N_HEADS, "cached keys are only supported for a single query block"
        past_len = k_past.shape[1]
        past_blk = min(past_len, SB_BLOCK)
        n_past = past_len // past_blk
        slab = past_blk * N_HEADS
        newest = pl.BlockSpec((None, slab, HEAD_DIM), lambda b, h: (b, n_past - 1, 0))
        hbm = pl.BlockSpec(memory_space=pl.ANY)
        in_specs += [newest, newest, hbm, hbm]
        kp = k_past.reshape(batch, past_len * N_HEADS, HEAD_DIM)
        vp = v_past.reshape(batch, past_len * N_HEADS, HEAD_DIM)
        args += [kp, vp, kp, vp]
        scratch = [pltpu.VMEM((slab, HEAD_DIM), F32), pltpu.VMEM((slab, HEAD_DIM), F32),
                   pltpu.SemaphoreType.DMA((2,))]
    return pl.pallas_call(
        functools.partial(_sb_body, blk, n_q, heads, past_blk, n_past),
        out_shape=jax.ShapeDtypeStruct((n, d), BF16),
        grid=(batch, N_HEADS // heads), in_specs=in_specs, out_specs=tok,
        scratch_shapes=scratch,
        compiler_params=_params("parallel", "parallel"), name="stick_breaking")(*args)


def _twice_gelu_tanh(x):
    t = jnp.tanh(x * (0.7978845608028654 + (0.7978845608028654 * 0.044715) * (x * x)))
    return x + x * t


def _gmlp_body(length, n_sub, emit_v, h_ref, w_ref, g_ref, b_ref, ws_ref, bs_ref, y_ref, *v_out):
    h = h_ref[...]
    d_c = w_ref.shape[1] // 2
    dg = d_c // GMLP_GROUPS
    i = lax.broadcasted_iota(jnp.int32, (length, length), 0)
    j = lax.broadcasted_iota(jnp.int32, (length, length), 1)
    mask = (j // CHUNK) <= (i // CHUNK)
    half_bias = 0.5 * bs_ref[...]

    vs = [_twice_gelu_tanh(_dot(h, w_ref[:, d_c + g * dg:d_c + (g + 1) * dg])) for g in range(GMLP_GROUPS)]
    total = functools.reduce(jnp.add, [jnp.sum(v, axis=-1, keepdims=True) for v in vs])
    total_sq = functools.reduce(jnp.add, [jnp.sum(v * v, axis=-1, keepdims=True) for v in vs])
    mean = total / d_c
    inv = lax.rsqrt(total_sq / d_c - mean * mean + 4.0 * EPS)
    for g in range(GMLP_GROUPS):
        cols = slice(g * dg, (g + 1) * dg)
        vn = (vs[g] - mean) * inv * g_ref[:, cols] + b_ref[:, cols]
        if emit_v:
            v_out[0][:, cols] = vn
        vn = vn.astype(BF16)
        u = _twice_gelu_tanh(_dot(h, w_ref[:, cols]))
        w = jnp.where(mask, 0.5 * ws_ref[g], 0.0).astype(BF16)
        for c in range(n_sub):
            rows = slice(c * length, (c + 1) * length)
            mixed = _dot(w, vn[rows]) + half_bias[:, g:g + 1]
            y_ref[rows, cols] = (u[rows] * mixed).astype(BF16)


def _gmlp(h, first_tile, n, w_in, ln_g, ln_b, w_s, b_s, length, tile, emit_v):
    d = h.shape[1]
    d_c = w_in[0].shape[-1] // 2
    bs = b_s.T
    out_shape = [jax.ShapeDtypeStruct((n, d_c), BF16)]
    out_specs = [_rows(tile, d_c)]
    if emit_v:
        out_shape.append(jax.ShapeDtypeStruct((n, d_c), F32))
        out_specs.append(_rows(tile, d_c))
    outs = pl.pallas_call(
        functools.partial(_gmlp_body, length, tile // length, emit_v),
        out_shape=out_shape, grid=(n // tile,),
        in_specs=[_rows(tile, d, first_tile), w_in[1], ln_g[1], ln_b[1], _resident(w_s.shape), _resident(bs.shape)],
        out_specs=out_specs,
        compiler_params=_params("parallel"), name="gmlp")(h, w_in[0], ln_g[0], ln_b[0], w_s, bs)
    return (outs[0], outs[1]) if emit_v else (outs[0], None)


def _trunk(streams, p):
    d = streams[0]['x'].shape[-1]
    tile = TOKEN_TILE
    depth = p['ffn1_norm'].shape[0]
    for st in streams:
        st['batch'], st['seq'], _ = st['x'].shape
        st['n'] = st['batch'] * st['seq']
        assert st['n'] % tile == 0
        st['new_k'], st['new_v'], st['new_s'], st['new_cv'] = [], [], [], []
    first_tiles = [sum(st['n'] for st in streams[:i]) // tile for i in range(len(streams))]

    def row(name, l):
        v = p[name]
        return _layer(v.reshape(v.shape[0], 1, -1), l)

    def together(arrays):
        return [(a, 0, a.shape[0] // tile) for a in arrays]

    x = together([st['x'].reshape(st['n'], d) for st in streams])
    pre = []
    for l in range(depth):
        xa, h = _ffn(x, pre, row('ffn1_norm', l), _layer(p['ffn1_w_gate'], l), _layer(p['ffn1_w_up'], l),
                     _layer(p['ffn1_w_down'], l), row('mix_norm', l), tile)
        x = _whole(xa, tile)
        if l % 2 == 0:
            e = l // 2
            mix_a, mix_b = [], []
            for st, first in zip(streams, first_tiles):
                *hgrn_in, qb, kb, vb, k, v = _abin(h, first, st['n'], _layer(p['ab_w_in'], e), p['ab_lb'],
                                                   row('ab_g_q', e), row('ab_g_k', e), e, tile)
                state = None if st['hgrn_state'] is None else st['hgrn_state'][e]
                ma, s_new = _hgrn(hgrn_in, state, p['ab_g_out'][e], p['ab_lb'], e, st['batch'], st['seq'])
                mix_a.append(ma)
                mix_b.append(_stick_breaking(qb, kb, vb, None if st['sb_k'] is None else st['sb_k'][e],
                                             None if st['sb_v'] is None else st['sb_v'][e], st['batch'], st['seq']))
                st['new_k'].append(k.reshape(st['batch'], st['seq'], N_HEADS, HEAD_DIM))
                st['new_v'].append(v.reshape(st['batch'], st['seq'], N_HEADS, HEAD_DIM))
                st['new_s'].append(s_new)
            pre = [(together(mix_a), _layer(p['ab_w_out'], e, 0, 2)), (together(mix_b), _layer(p['ab_w_out'], e, 1, 2))]
        else:
            o = l // 2
            ys = []
            for st, first in zip(streams, first_tiles):
                length = min(st['seq'], GMLP_LEN)
                y, cv = _gmlp(h, first, st['n'], _layer(p['c_w_in'], o), row('c_ln_g', o), row('c_ln_b', o),
                              p['c_w_s'][o, :, :length, :length], p['c_b_s'][o, :, :length], length, tile,
                              st['want_cv'])
                ys.append(y)
                if st['want_cv']:
                    st['new_cv'].append(cv.reshape(st['batch'], st['seq'], -1))
            pre = [(together(ys), _layer(p['c_w_out'], o))]
        if l + 1 < depth:
            xa, _ = _ffn(x, pre, row('ffn2_norm', l), _layer(p['ffn2_w_gate'], l), _layer(p['ffn2_w_up'], l),
                         _layer(p['ffn2_w_down'], l), None, tile)
            x = _whole(xa, tile)
            pre = []
    l = depth - 1
    outs = []
    for i, (st, first) in enumerate(zip(streams, first_tiles)):
        xi = [(x[0][0], first, st['n'] // tile)]
        pre_i = [([m[i]], w) for m, w in pre]
        y, _ = _ffn(xi, pre_i, row('ffn2_norm', l), _layer(p['ffn2_w_gate'], l), _layer(p['ffn2_w_up'], l),
                    _layer(p['ffn2_w_down'], l), None, tile)
        outs.append(y.reshape(st['batch'], st['seq'], d))
    return outs


_MATMUL_WEIGHTS = ('ffn1_w_gate', 'ffn1_w_up', 'ffn1_w_down', 'ffn2_w_gate', 'ffn2_w_up',
                   'ffn2_w_down', 'ab_w_in', 'ab_w_out', 'c_w_in', 'c_w_out')


def kernel(x_prompt, x_sample, cache_sb_k, cache_sb_v, state_hgrn, ffn1_norm, ffn1_w_gate, ffn1_w_up, ffn1_w_down, mix_norm, ffn2_norm, ffn2_w_gate, ffn2_w_up, ffn2_w_down, ab_w_in, ab_lb, ab_g_out, ab_g_q, ab_g_k, ab_w_out, c_w_in, c_ln_g, c_ln_b, c_w_s, c_b_s, c_w_out):
    p = {'ffn1_norm': ffn1_norm, 'ffn1_w_gate': ffn1_w_gate, 'ffn1_w_up': ffn1_w_up, 'ffn1_w_down': ffn1_w_down,
         'mix_norm': mix_norm, 'ffn2_norm': ffn2_norm, 'ffn2_w_gate': ffn2_w_gate, 'ffn2_w_up': ffn2_w_up,
         'ffn2_w_down': ffn2_w_down, 'ab_w_in': ab_w_in, 'ab_lb': ab_lb, 'ab_g_out': ab_g_out, 'ab_g_q': ab_g_q,
         'ab_g_k': ab_g_k, 'ab_w_out': ab_w_out, 'c_w_in': c_w_in, 'c_ln_g': c_ln_g, 'c_ln_b': c_ln_b,
         'c_w_s': c_w_s, 'c_b_s': c_b_s, 'c_w_out': c_w_out}
    for name in _MATMUL_WEIGHTS:
        p[name] = p[name].astype(BF16)
    prompt = {'x': x_prompt, 'hgrn_state': None, 'sb_k': None, 'sb_v': None, 'want_cv': False}
    sample = {'x': x_sample, 'hgrn_state': state_hgrn, 'sb_k': cache_sb_k, 'sb_v': cache_sb_v, 'want_cv': True}
    y_prompt, y_sample = _trunk([prompt, sample], p)
    return (y_prompt, y_sample, jnp.stack(prompt['new_k']), jnp.stack(prompt['new_v']), jnp.stack(prompt['new_s']),
            jnp.stack(sample['new_k']), jnp.stack(sample['new_v']), jnp.stack(sample['new_s']),
            jnp.stack(sample['new_cv']))
```

```python
import functools

import jax
import jax.numpy as jnp
from jax import lax
from jax.experimental import pallas as pl
from jax.experimental.pallas import tpu as pltpu

F32 = jnp.float32
BF16 = jnp.bfloat16

EPS = 1e-6
FFN_SCALE = 0.5
HEAD_DIM = 128
N_HEADS = 8
CHUNK = 64
GMLP_LEN = 128
GMLP_GROUPS = 8
V7X_VMEM_LIMIT_BYTES = 56 * 1024 * 1024
TOKEN_TILE = 512
GMLP_TOKEN_TILE = 1024
FFN_ROW_GROUPS = 2
AB_ROW_GROUPS = 2
HGRN_CHUNK = 128
HGRN_HEADS_PER_STEP = 4
HGRN_SAFE_EXPONENT = 60.0
SB_BLOCK = 256
SB_HEADS_PER_STEP = 4
SB_EXIT = -105.0


def _params(*sem):
    return pltpu.CompilerParams(dimension_semantics=sem, vmem_limit_bytes=V7X_VMEM_LIMIT_BYTES)


def _resident(shape):
    zeros = (0,) * len(shape)
    return pl.BlockSpec(shape, lambda *_: zeros, pipeline_mode=pl.Buffered(1))


def _layer(stacked, layer, row_block=0, n_row_blocks=1):
    rows = stacked.shape[1] // n_row_blocks
    block = (None, rows) + stacked.shape[2:]
    index = (layer, row_block) + (0,) * (stacked.ndim - 2)
    return stacked, pl.BlockSpec(block, lambda *_: index, pipeline_mode=pl.Buffered(1))


def _rows(tile, width, first_tile=0):
    return pl.BlockSpec((tile, width), lambda i: (i + first_tile, 0))


def _whole(arr, tile):
    return [(arr, 0, arr.shape[0] // tile)]


def _segment_index(start, first_tile, n_tiles, i):
    return jnp.clip(i - start, 0, n_tiles - 1) + first_tile, 0


def _segment_specs(segments, tile, total_tiles):
    specs, start = [], 0
    for arr, first_tile, n_tiles in segments:
        specs.append(pl.BlockSpec((tile, arr.shape[1]),
                                  functools.partial(_segment_index, start, first_tile, n_tiles)))
        start += n_tiles
    assert start == total_tiles and len(segments) <= 2
    return specs


def _segment_rows(refs, n_first, rows):
    if len(refs) == 1:
        return refs[0][rows, :]
    return jnp.where(pl.program_id(0) < n_first, refs[0][rows, :], refs[1][rows, :])


def _sigmoid(x):
    return 1.0 / (1.0 + jnp.exp(-x))


def _rms_rows(x, g):
    ms = jnp.mean(x * x, axis=-1, keepdims=True)
    return x * lax.rsqrt(ms + EPS) * g


def _dot(a, b):
    return jnp.dot(a, b, preferred_element_type=F32)


def _dot_nt(a, b):
    return lax.dot_general(a, b, (((1,), (1,)), ((), ())), preferred_element_type=F32)


def _dot_tn(a, b):
    return lax.dot_general(a, b, (((0,), (0,)), ((), ())), preferred_element_type=F32)


def _split_bf16(x):
    hi = x.astype(BF16)
    lo = (x - hi.astype(F32)).astype(BF16)
    return hi, lo


def _head_rows(head, n):
    return pl.ds(head, n, stride=N_HEADS)


def _ffn_body(seg_counts, n_first, has_next, *refs):
    pos = seg_counts[0]
    x_refs = refs[:pos]
    pre = []
    for count in seg_counts[1:]:
        pre.append((refs[pos:pos + count], refs[pos + count]))
        pos += count + 1
    g_ref, wg_ref, wu_ref, wd_ref = refs[pos:pos + 4]
    pos += 4
    g2_ref = refs[pos] if has_next else None
    pos += int(has_next)
    xo_ref = refs[pos]
    ho_ref = refs[pos + 1] if has_next else None

    tile = xo_ref.shape[0]
    parts = [slice(k * (tile // FFN_ROW_GROUPS), (k + 1) * (tile // FFN_ROW_GROUPS)) for k in range(FFN_ROW_GROUPS)]
    xs = [_segment_rows(x_refs, n_first, r) for r in parts]
    for m_refs, w_ref in pre:
        xs = [x + _dot(_segment_rows(m_refs, n_first, r), w_ref[...]) for x, r in zip(xs, parts)]
    hs = [_rms_rows(x, g_ref[...]).astype(BF16) for x in xs]
    gates = [_dot(h, wg_ref[...]) for h in hs]
    ups = [_dot(h, wu_ref[...]) for h in hs]
    acts = [(a * _sigmoid(a) * b).astype(BF16) for a, b in zip(gates, ups)]
    downs = [_dot(act, wd_ref[...]) for act in acts]
    for x, y, r in zip(xs, downs, parts):
        x = x + FFN_SCALE * y
        xo_ref[r, :] = x
        if has_next:
            ho_ref[r, :] = _rms_rows(x, g2_ref[...]).astype(BF16)


def _ffn(x, pre, g, wg, wu, wd, g_next, tile):
    n_tiles = sum(seg[2] for seg in x)
    d = x[0][0].shape[1]
    has_next = g_next is not None
    operands = [x] + [m for m, _ in pre]
    n_first = max(segs[0][2] if len(segs) == 2 else 0 for segs in operands)
    assert all(len(segs) == 1 or segs[0][2] == n_first for segs in operands)
    in_specs = _segment_specs(x, tile, n_tiles)
    args = [seg[0] for seg in x]
    for m, (w, w_spec) in pre:
        in_specs += _segment_specs(m, tile, n_tiles) + [w_spec]
        args += [seg[0] for seg in m] + [w]
    for arr, spec in (g, wg, wu, wd):
        in_specs.append(spec)
        args.append(arr)
    out_shape = [jax.ShapeDtypeStruct((n_tiles * tile, d), F32)]
    out_specs = [_rows(tile, d)]
    if has_next:
        in_specs.append(g_next[1])
        args.append(g_next[0])
        out_shape.append(jax.ShapeDtypeStruct((n_tiles * tile, d), BF16))
        out_specs.append(_rows(tile, d))
    outs = pl.pallas_call(
        functools.partial(_ffn_body, tuple(len(segs) for segs in operands), n_first, has_next),
        out_shape=out_shape, grid=(n_tiles,), in_specs=in_specs, out_specs=out_specs,
        compiler_params=_params("parallel"), name="ffn")(*args)
    return (outs[0], outs[1]) if has_next else (outs[0], None)


def _abin_body(layer, h_ref, w_ref, lbp_ref, gq_ref, gk_ref,
               qa_ref, lf_ref, va_ref, ga_ref, qb_ref, kb_ref, vb_ref, k_ref, v_ref):
    tile, d = h_ref.shape
    group = tile // AB_ROW_GROUPS

    lbp = lbp_ref[...]
    e = jnp.exp(lbp - jnp.max(lbp, axis=0, keepdims=True))
    lb = jnp.sum(e[:layer + 1], axis=0, keepdims=True) / jnp.sum(e, axis=0, keepdims=True)

    for k in range(AB_ROW_GROUPS):
        r = slice(k * group, (k + 1) * group)
        h = h_ref[r, :]

        def seg(i, h=h):
            return _dot(h, w_ref[:, i * d:(i + 1) * d])

        qa = seg(0)
        qa_ref[r, :] = (qa * _sigmoid(qa)).astype(BF16)
        f = lb + (1.0 - lb) * _sigmoid(seg(1))
        lf_ref[r, :] = jnp.log(f)
        va_ref[r, :] = seg(2).astype(BF16)
        ga_ref[r, :] = _sigmoid(seg(3)).astype(BF16)

        qb = seg(4)
        kb = seg(5)
        vb = seg(6)
        vb_ref[r, :] = vb.astype(BF16)
        for hh in range(N_HEADS):
            sl = slice(hh * HEAD_DIM, (hh + 1) * HEAD_DIM)
            qb_ref[r, sl] = (_rms_rows(qb[:, sl], gq_ref[...]) * HEAD_DIM ** -0.5).astype(BF16)
            kn = _rms_rows(kb[:, sl], gk_ref[...])
            kb_ref[r, sl] = kn.astype(BF16)
            head_rows = pl.ds(k * group * N_HEADS + hh, group, stride=N_HEADS)
            k_ref[head_rows, :] = kn
            v_ref[head_rows, :] = vb[:, sl]


def _abin(h, first_tile, n, w_in, lb_param, g_q, g_k, layer, tile):
    d = h.shape[1]
    assert d == N_HEADS * HEAD_DIM
    feat = [BF16, F32, BF16, BF16, BF16, BF16, BF16]
    out_shape = [jax.ShapeDtypeStruct((n, d), t) for t in feat]
    out_specs = [_rows(tile, d) for _ in feat]
    for _ in range(2):
        out_shape.append(jax.ShapeDtypeStruct((n * N_HEADS, HEAD_DIM), F32))
        out_specs.append(_rows(tile * N_HEADS, HEAD_DIM))
    return pl.pallas_call(
        functools.partial(_abin_body, layer),
        out_shape=out_shape, grid=(n // tile,),
        in_specs=[_rows(tile, d, first_tile), w_in[1], _resident(lb_param.shape), g_q[1], g_k[1]],
        out_specs=out_specs,
        compiler_params=_params("parallel"), name="ab_in_proj")(
            h, w_in[0], lb_param, g_q[0], g_k[0])


def _hgrn_scores_any_decay(q, k, lf2, chunk):
    row = lax.broadcasted_iota(jnp.int32, (chunk, chunk), 0)
    col = lax.broadcasted_iota(jnp.int32, (chunk, chunk), 1)
    t1 = lax.broadcasted_iota(jnp.int32, (chunk, 1), 0)
    scores = jnp.where(row == col, jnp.sum(q * k, axis=-1, keepdims=True), 0.0)
    half = chunk // 2
    while half >= 1:
        blk = 2 * half
        same = (row // blk) == (col // blk)
        mid = (row // blk) * blk + half
        right = row >= mid
        seg = same & ((right & (col >= mid) & (col <= row)) | ((col > row) & (col < mid)))
        d2 = _dot(jnp.where(seg, 1.0, 0.0).astype(BF16), lf2)
        decay = jnp.exp(d2[:, :HEAD_DIM] + d2[:, HEAD_DIM:])
        right1 = (t1 % blk) >= half
        qe = jnp.where(right1, q * decay, 0.0).astype(BF16)
        ke = jnp.where(right1, 0.0, k * decay).astype(BF16)
        scores = scores + jnp.where(same, _dot_nt(qe, ke), 0.0)
        half //= 2
    return scores


def _hgrn_body(chunk, n_chunks, heads, has_s0, factored, *refs):
    q_ref, lf_ref, v_ref, gt_ref = refs[:4]
    pos = 4
    s0_ref = refs[pos] if has_s0 else None
    pos += int(has_s0)
    go_ref, o_ref, so_ref = refs[pos:pos + 3]

    row = lax.broadcasted_iota(jnp.int32, (chunk, chunk), 0)
    col = lax.broadcasted_iota(jnp.int32, (chunk, chunk), 1)
    causal = col <= row
    lower = jnp.where(causal, 1.0, 0.0).astype(BF16)
    mid = chunk // 2
    head_cols = [slice(j * HEAD_DIM, (j + 1) * HEAD_DIM) for j in range(heads)]

    def step(i, states):
        rows = pl.ds(pl.multiple_of(i * chunk, chunk), chunk)
        lfs = [lf_ref[rows, cl] for cl in head_cols]
        lf2s = [jnp.concatenate(_split_bf16(lf), axis=1) for lf in lfs]
        c2s = [_dot(lower, lf2) for lf2 in lf2s]
        qs = [q_ref[rows, cl].astype(F32) for cl in head_cols]
        ks = [1.0 - jnp.exp(lf) for lf in lfs]
        cs = [c2[:, :HEAD_DIM] + c2[:, HEAD_DIM:] for c2 in c2s]
        c_lasts = [c[chunk - 1:chunk, :] for c in cs]
        vs = [v_ref[rows, cl] for cl in head_cols]
        if factored:
            c_mids = [c[mid - 1:mid, :] for c in cs]
            rel = [c - cm for c, cm in zip(cs, c_mids)]
            qes = [(q * jnp.exp(a)).astype(BF16) for q, a in zip(qs, rel)]
            kes = [(k * jnp.exp(-a)).astype(BF16) for k, a in zip(ks, rel)]
            rhs = [jnp.concatenate([ke, (st * jnp.exp(cm)).astype(BF16)], axis=0)
                   for ke, st, cm in zip(kes, states, c_mids)]
            both = [_dot_nt(qe, r) for qe, r in zip(qes, rhs)]
            grown = [_dot_tn(v, ke) * jnp.exp(cl - cm) for v, ke, cl, cm in zip(vs, kes, c_lasts, c_mids)]
            scores = [jnp.where(causal, bt[:, :chunk], 0.0) for bt in both]
            inter = [bt[:, chunk:] for bt in both]
        else:
            inter = [_dot_nt((q * jnp.exp(c)).astype(BF16), st.astype(BF16)) for q, c, st in zip(qs, cs, states)]
            scores = [_hgrn_scores_any_decay(q, k, lf2, chunk) for q, k, lf2 in zip(qs, ks, lf2s)]
            kds = [(k * jnp.exp(cl - c)).astype(BF16) for k, cl, c in zip(ks, c_lasts, cs)]
            grown = [_dot_tn(v, kd) for v, kd in zip(vs, kds)]
        intra = [_dot(s.astype(BF16), v) for s, v in zip(scores, vs)]
        for j, cl in enumerate(head_cols):
            o = _rms_rows(inter[j] + intra[j], go_ref[j]) * gt_ref[rows, cl].astype(F32)
            o_ref[rows, cl] = o.astype(o_ref.dtype)
        return tuple(st * jnp.exp(cl) + g for st, cl, g in zip(states, c_lasts, grown))

    if has_s0:
        states = tuple(s0_ref[j].T for j in range(heads))
    else:
        states = tuple(jnp.zeros((HEAD_DIM, HEAD_DIM), F32) for _ in range(heads))
    states = lax.fori_loop(0, n_chunks, step, states, unroll=8) if n_chunks > 1 else step(0, states)
    for j in range(heads):
        so_ref[j] = states[j].T


def _hgrn_call(tokens, s0, g_out, batch, seq, chunk, heads, factored):
    n, d = tokens[0].shape
    has_s0 = s0 is not None
    width = heads * HEAD_DIM
    tok = pl.BlockSpec((seq, width), lambda b, h: (b, h))
    state = pl.BlockSpec((None, heads, HEAD_DIM, HEAD_DIM), lambda b, h: (b, h, 0, 0))
    in_specs = [tok] * len(tokens)
    args = list(tokens)
    if has_s0:
        in_specs.append(state)
        args.append(s0)
    in_specs.append(pl.BlockSpec((heads, 1, HEAD_DIM), lambda b, h: (h, 0, 0)))
    args.append(g_out.reshape(N_HEADS, 1, HEAD_DIM))
    return pl.pallas_call(
        functools.partial(_hgrn_body, chunk, seq // chunk, heads, has_s0, factored),
        out_shape=[jax.ShapeDtypeStruct((n, d), BF16),
                   jax.ShapeDtypeStruct((batch, N_HEADS, HEAD_DIM, HEAD_DIM), F32)],
        grid=(batch, N_HEADS // heads), in_specs=in_specs, out_specs=[tok, state],
        compiler_params=_params("parallel", "parallel"), name="hgrn2")(*args)


def _hgrn(tokens, s0, g_out, lb_param, layer, batch, seq):
    chunk = min(seq, HGRN_CHUNK)
    heads = N_HEADS if seq < HGRN_CHUNK else HGRN_HEADS_PER_STEP
    call = functools.partial(_hgrn_call, tokens, s0, g_out, batch, seq, chunk, heads)
    lb = jnp.cumsum(jax.nn.softmax(lb_param.astype(F32), axis=0), axis=0)[layer]
    reach = (chunk // 2) * jnp.max(-jnp.log(lb))
    return lax.cond(reach <= HGRN_SAFE_EXPONENT, lambda: call(True), lambda: call(False))


def _sb_logits_many(items):
    zs = [_dot_nt(q, kb) for q, kb, _, _ in items]
    sps = [jnp.maximum(z, 0.0) + jnp.log(1.0 + jnp.exp(-jnp.abs(z))) for z in zs]
    costs = [sp if it[3] is None else jnp.where(it[3], sp, 0.0) for sp, it in zip(sps, items)]
    suffix = [_dot(c.astype(BF16), it[2]) for c, it in zip(costs, items)]
    return [(z - sp - sf, jnp.sum(c, axis=-1, keepdims=True)) for z, sp, sf, c in zip(zs, sps, suffix, costs)]


def _sb_accumulate(logits, cost, vb, mask, acc, spent):
    w = jnp.exp(logits - spent)
    if mask is not None:
        w = jnp.where(mask, w, 0.0)
    return acc + _dot(w.astype(BF16), vb), spent + cost


def _strict_upper(n):
    j = lax.broadcasted_iota(jnp.int32, (n, n), 0)
    k = lax.broadcasted_iota(jnp.int32, (n, n), 1)
    return jnp.where(j > k, 1.0, 0.0).astype(BF16)


def _sb_body(blk, n_q, heads, past_blk, n_past, *refs):
    q_ref, k_ref, v_ref = refs[:3]
    if n_past:
        kp_new_ref, vp_new_ref, kp_hbm, vp_hbm, o_ref, kbuf, vbuf, sem = refs[3:]
    else:
        o_ref = refs[3]

    t = lax.broadcasted_iota(jnp.int32, (blk, blk), 0)
    s = lax.broadcasted_iota(jnp.int32, (blk, blk), 1)
    diag_mask = s < t
    upper = _strict_upper(blk)
    upper_past = _strict_upper(past_blk) if n_past else None
    head_cols = [slice(j * HEAD_DIM, (j + 1) * HEAD_DIM) for j in range(heads)]

    def still_live(spent):
        return functools.reduce(jnp.minimum, [jnp.min(r) for r in spent]) < -SB_EXIT

    for qi in range(n_q):
        rows = pl.ds(qi * blk, blk)
        qs = [q_ref[rows, c] for c in head_cols]
        n_older = qi + n_past
        items = [(qs[j], k_ref[rows, c], upper, diag_mask) for j, c in enumerate(head_cols)]
        values = [(v_ref[rows, c], diag_mask) for c in head_cols]
        if qi > 0:
            prev = pl.ds((qi - 1) * blk, blk)
            items += [(qs[j], k_ref[prev, c], upper, None) for j, c in enumerate(head_cols)]
            values += [(v_ref[prev, c], None) for c in head_cols]
        elif n_past:
            items += [(qs[j], kp_new_ref[_head_rows(j, past_blk), :].astype(BF16), upper_past, None)
                      for j in range(heads)]
            values += [(vp_new_ref[_head_rows(j, past_blk), :].astype(BF16), None) for j in range(heads)]
        logits = _sb_logits_many(items)
        accs = [jnp.zeros((blk, HEAD_DIM), F32) for _ in range(heads)]
        runs = [jnp.zeros((blk, 1), F32) for _ in range(heads)]
        for n, ((lg, stay), (vb, mask)) in enumerate(zip(logits, values)):
            j = n % heads
            accs[j], runs[j] = _sb_accumulate(lg, stay, vb, mask, accs[j], runs[j])

        if n_older > 1:
            def cond(carry):
                return (carry[0] < n_older) & carry[1]

            def body(carry, qi=qi, qs=qs):
                idx = carry[0]
                accs, runs = list(carry[2]), list(carry[3])
                if n_past:
                    src = pl.ds(pl.multiple_of((n_past - 1 - idx) * past_blk * N_HEADS, past_blk * N_HEADS),
                                past_blk * N_HEADS)
                    b = pl.program_id(0)
                    copies = [pltpu.make_async_copy(kp_hbm.at[b, src, :], kbuf, sem.at[0]),
                              pltpu.make_async_copy(vp_hbm.at[b, src, :], vbuf, sem.at[1])]
                    for cp in copies:
                        cp.start()
                    for cp in copies:
                        cp.wait()
                if n_past:
                    items = [(qs[j], kbuf[_head_rows(j, past_blk), :].astype(BF16), upper_past, None)
                             for j in range(heads)]
                    vbs = [vbuf[_head_rows(j, past_blk), :].astype(BF16) for j in range(heads)]
                else:
                    kr = pl.ds(pl.multiple_of((qi - 1 - idx) * blk, blk), blk)
                    items = [(qs[j], k_ref[kr, c], upper, None) for j, c in enumerate(head_cols)]
                    vbs = [v_ref[kr, c] for c in head_cols]
                for j, (lg, stay) in enumerate(_sb_logits_many(items)):
                    accs[j], runs[j] = _sb_accumulate(lg, stay, vbs[j], None, accs[j], runs[j])
                return idx + 1, still_live(runs), tuple(accs), tuple(runs)

            _, _, accs, _ = lax.while_loop(cond, body, (jnp.int32(1), still_live(runs), tuple(accs), tuple(runs)))
        for j, c in enumerate(head_cols):
            o_ref[rows, c] = accs[j].astype(o_ref.dtype)


def _stick_breaking(qn, kb, vb, k_past, v_past, batch, seq):
    n, d = qn.shape
    blk = min(seq, SB_BLOCK)
    n_q = seq // blk
    heads = N_HEADS if seq < SB_BLOCK else SB_HEADS_PER_STEP
    tok = pl.BlockSpec((seq, heads * HEAD_DIM), lambda b, h: (b, h))
    in_specs = [tok, tok, tok]
    args = [qn, kb, vb]
    scratch = []
    past_blk = n_past = 0
    if k_past is not None:
        assert n_q == 1 and heads == N_HEADS, "cached keys are only supported for a single query block"
        past_len = k_past.shape[1]
        past_blk = min(past_len, SB_BLOCK)
        n_past = past_len // past_blk
        slab = past_blk * N_HEADS
        newest = pl.BlockSpec((None, slab, HEAD_DIM), lambda b, h: (b, n_past - 1, 0))
        hbm = pl.BlockSpec(memory_space=pl.ANY)
        in_specs += [newest, newest, hbm, hbm]
        kp = k_past.reshape(batch, past_len * N_HEADS, HEAD_DIM)
        vp = v_past.reshape(batch, past_len * N_HEADS, HEAD_DIM)
        args += [kp, vp, kp, vp]
        scratch = [pltpu.VMEM((slab, HEAD_DIM), F32), pltpu.VMEM((slab, HEAD_DIM), F32),
                   pltpu.SemaphoreType.DMA((2,))]
    return pl.pallas_call(
        functools.partial(_sb_body, blk, n_q, heads, past_blk, n_past),
        out_shape=jax.ShapeDtypeStruct((n, d), BF16),
        grid=(batch, N_HEADS // heads), in_specs=in_specs, out_specs=tok,
        scratch_shapes=scratch,
        compiler_params=_params("parallel", "parallel"), name="stick_breaking")(*args)


def _twice_gelu_tanh(x):
    t = jnp.tanh(x * (0.7978845608028654 + (0.7978845608028654 * 0.044715) * (x * x)))
    return x + x * t


def _gmlp_body(length, n_sub, emit_v, h_ref, w_ref, g_ref, b_ref, ws_ref, bs_ref, y_ref, *v_out):
    h = h_ref[...]
    d_c = w_ref.shape[1] // 2
    dg = d_c // GMLP_GROUPS
    i = lax.broadcasted_iota(jnp.int32, (length, length), 0)
    j = lax.broadcasted_iota(jnp.int32, (length, length), 1)
    mask = (j // CHUNK) <= (i // CHUNK)
    half_bias = 0.5 * bs_ref[...]

    vs = [_twice_gelu_tanh(_dot(h, w_ref[:, d_c + g * dg:d_c + (g + 1) * dg])) for g in range(GMLP_GROUPS)]
    total = functools.reduce(jnp.add, [jnp.sum(v, axis=-1, keepdims=True) for v in vs])
    total_sq = functools.reduce(jnp.add, [jnp.sum(v * v, axis=-1, keepdims=True) for v in vs])
    mean = total / d_c
    inv = lax.rsqrt(total_sq / d_c - mean * mean + 4.0 * EPS)
    for g in range(GMLP_GROUPS):
        cols = slice(g * dg, (g + 1) * dg)
        vn = (vs[g] - mean) * inv * g_ref[:, cols] + b_ref[:, cols]
        if emit_v:
            v_out[0][:, cols] = vn
        vn = vn.astype(BF16)
        u = _twice_gelu_tanh(_dot(h, w_ref[:, cols]))
        w = jnp.where(mask, 0.5 * ws_ref[g], 0.0).astype(BF16)
        for c in range(n_sub):
            rows = slice(c * length, (c + 1) * length)
            mixed = _dot(w, vn[rows]) + half_bias[:, g:g + 1]
            y_ref[rows, cols] = (u[rows] * mixed).astype(BF16)


def _gmlp(h, first_tile, n, w_in, ln_g, ln_b, w_s, b_s, length, tile, emit_v):
    d = h.shape[1]
    d_c = w_in[0].shape[-1] // 2
    bs = b_s.T
    out_shape = [jax.ShapeDtypeStruct((n, d_c), BF16)]
    out_specs = [_rows(tile, d_c)]
    if emit_v:
        out_shape.append(jax.ShapeDtypeStruct((n, d_c), F32))
        out_specs.append(_rows(tile, d_c))
    outs = pl.pallas_call(
        functools.partial(_gmlp_body, length, tile // length, emit_v),
        out_shape=out_shape, grid=(n // tile,),
        in_specs=[_rows(tile, d, first_tile), w_in[1], ln_g[1], ln_b[1], _resident(w_s.shape), _resident(bs.shape)],
        out_specs=out_specs,
        compiler_params=_params("parallel"), name="gmlp")(h, w_in[0], ln_g[0], ln_b[0], w_s, bs)
    return (outs[0], outs[1]) if emit_v else (outs[0], None)


def _trunk(streams, p):
    d = streams[0]['x'].shape[-1]
    tile = TOKEN_TILE
    depth = p['ffn1_norm'].shape[0]
    for st in streams:
        st['batch'], st['seq'], _ = st['x'].shape
        st['n'] = st['batch'] * st['seq']
        assert st['n'] % tile == 0
        st['new_k'], st['new_v'], st['new_s'], st['new_cv'] = [], [], [], []
    first_tiles = [sum(st['n'] for st in streams[:i]) // tile for i in range(len(streams))]

    def row(name, l):
        v = p[name]
        return _layer(v.reshape(v.shape[0], 1, -1), l)

    def together(arrays):
        return [(a, 0, a.shape[0] // tile) for a in arrays]

    x = together([st['x'].reshape(st['n'], d) for st in streams])
    pre = []
    for l in range(depth):
        xa, h = _ffn(x, pre, row('ffn1_norm', l), _layer(p['ffn1_w_gate'], l), _layer(p['ffn1_w_up'], l),
                     _layer(p['ffn1_w_down'], l), row('mix_norm', l), tile)
        x = _whole(xa, tile)
        if l % 2 == 0:
            e = l // 2
            mix_a, mix_b = [], []
            for st, first in zip(streams, first_tiles):
                *hgrn_in, qb, kb, vb, k, v = _abin(h, first, st['n'], _layer(p['ab_w_in'], e), p['ab_lb'],
                                                   row('ab_g_q', e), row('ab_g_k', e), e, tile)
                state = None if st['hgrn_state'] is None else st['hgrn_state'][e]
                ma, s_new = _hgrn(hgrn_in, state, p['ab_g_out'][e], p['ab_lb'], e, st['batch'], st['seq'])
                mix_a.append(ma)
                mix_b.append(_stick_breaking(qb, kb, vb, None if st['sb_k'] is None else st['sb_k'][e],
                                             None if st['sb_v'] is None else st['sb_v'][e], st['batch'], st['seq']))
                st['new_k'].append(k.reshape(st['batch'], st['seq'], N_HEADS, HEAD_DIM))
                st['new_v'].append(v.reshape(st['batch'], st['seq'], N_HEADS, HEAD_DIM))
                st['new_s'].append(s_new)
            pre = [(together(mix_a), _layer(p['ab_w_out'], e, 0, 2)), (together(mix_b), _layer(p['ab_w_out'], e, 1, 2))]
        else:
            o = l // 2
            ys = []
            for st, first in zip(streams, first_tiles):
                length = min(st['seq'], GMLP_LEN)
                assert st['n'] % GMLP_TOKEN_TILE == 0 and (first * tile) % GMLP_TOKEN_TILE == 0
                y, cv = _gmlp(h, first * tile // GMLP_TOKEN_TILE, st['n'], _layer(p['c_w_in'], o),
                              row('c_ln_g', o), row('c_ln_b', o), p['c_w_s'][o, :, :length, :length],
                              p['c_b_s'][o, :, :length], length, GMLP_TOKEN_TILE, st['want_cv'])
                ys.append(y)
                if st['want_cv']:
                    st['new_cv'].append(cv.reshape(st['batch'], st['seq'], -1))
            pre = [(together(ys), _layer(p['c_w_out'], o))]
        if l + 1 < depth:
            xa, _ = _ffn(x, pre, row('ffn2_norm', l), _layer(p['ffn2_w_gate'], l), _layer(p['ffn2_w_up'], l),
                         _layer(p['ffn2_w_down'], l), None, tile)
            x = _whole(xa, tile)
            pre = []
    l = depth - 1
    outs = []
    for i, (st, first) in enumerate(zip(streams, first_tiles)):
        xi = [(x[0][0], first, st['n'] // tile)]
        pre_i = [([m[i]], w) for m, w in pre]
        y, _ = _ffn(xi, pre_i, row('ffn2_norm', l), _layer(p['ffn2_w_gate'], l), _layer(p['ffn2_w_up'], l),
                    _layer(p['ffn2_w_down'], l), None, tile)
        outs.append(y.reshape(st['batch'], st['seq'], d))
    return outs


_MATMUL_WEIGHTS = ('ffn1_w_gate', 'ffn1_w_up', 'ffn1_w_down', 'ffn2_w_gate', 'ffn2_w_up',
                   'ffn2_w_down', 'ab_w_in', 'ab_w_out', 'c_w_in', 'c_w_out')


def kernel(x_prompt, x_sample, cache_sb_k, cache_sb_v, state_hgrn, ffn1_norm, ffn1_w_gate, ffn1_w_up, ffn1_w_down, mix_norm, ffn2_norm, ffn2_w_gate, ffn2_w_up, ffn2_w_down, ab_w_in, ab_lb, ab_g_out, ab_g_q, ab_g_k, ab_w_out, c_w_in, c_ln_g, c_ln_b, c_w_s, c_b_s, c_w_out):
    p = {'ffn1_norm': ffn1_norm, 'ffn1_w_gate': ffn1_w_gate, 'ffn1_w_up': ffn1_w_up, 'ffn1_w_down': ffn1_w_down,
         'mix_norm': mix_norm, 'ffn2_norm': ffn2_norm, 'ffn2_w_gate': ffn2_w_gate, 'ffn2_w_up': ffn2_w_up,
         'ffn2_w_down': ffn2_w_down, 'ab_w_in': ab_w_in, 'ab_lb': ab_lb, 'ab_g_out': ab_g_out, 'ab_g_q': ab_g_q,
         'ab_g_k': ab_g_k, 'ab_w_out': ab_w_out, 'c_w_in': c_w_in, 'c_ln_g': c_ln_g, 'c_ln_b': c_ln_b,
         'c_w_s': c_w_s, 'c_b_s': c_b_s, 'c_w_out': c_w_out}
    for name in _MATMUL_WEIGHTS:
        p[name] = p[name].astype(BF16)
    prompt = {'x': x_prompt, 'hgrn_state': None, 'sb_k': None, 'sb_v': None, 'want_cv': False}
    sample = {'x': x_sample, 'hgrn_state': state_hgrn, 'sb_k': cache_sb_k, 'sb_v': cache_sb_v, 'want_cv': True}
    y_prompt, y_sample = _trunk([prompt, sample], p)
    return (y_prompt, y_sample, jnp.stack(prompt['new_k']), jnp.stack(prompt['new_v']), jnp.stack(prompt['new_s']),
            jnp.stack(sample['new_k']), jnp.stack(sample['new_v']), jnp.stack(sample['new_s']),
            jnp.stack(sample['new_cv']))
```

```python
import functools

import jax
import jax.numpy as jnp
from jax import lax
from jax.experimental import pallas as pl
from jax.experimental.pallas import tpu as pltpu

F32 = jnp.float32
BF16 = jnp.bfloat16

EPS = 1e-6
FFN_SCALE = 0.5
HEAD_DIM = 128
N_HEADS = 8
CHUNK = 64
GMLP_LEN = 128
GMLP_GROUPS = 8
V7X_VMEM_LIMIT_BYTES = 56 * 1024 * 1024
TOKEN_TILE = 512
GMLP_TOKEN_TILE = 1024
FFN_ROW_GROUPS = 2
AB_ROW_GROUPS = 2
HGRN_CHUNK = 128
HGRN_HEADS_PER_STEP = 4
HGRN_UNROLL = 8
HGRN_SAFE_EXPONENT = 60.0
SB_BLOCK = 256
SB_HEADS_PER_STEP = 4
SB_EXIT = -105.0


def _params(*sem):
    return pltpu.CompilerParams(dimension_semantics=sem, vmem_limit_bytes=V7X_VMEM_LIMIT_BYTES)


def _resident(shape):
    zeros = (0,) * len(shape)
    return pl.BlockSpec(shape, lambda *_: zeros, pipeline_mode=pl.Buffered(1))


def _layer(stacked, layer, row_block=0, n_row_blocks=1):
    rows = stacked.shape[1] // n_row_blocks
    block = (None, rows) + stacked.shape[2:]
    index = (layer, row_block) + (0,) * (stacked.ndim - 2)
    return stacked, pl.BlockSpec(block, lambda *_: index, pipeline_mode=pl.Buffered(1))


def _rows(tile, width, first_tile=0):
    return pl.BlockSpec((tile, width), lambda i: (i + first_tile, 0))


def _whole(arr, tile):
    return [(arr, 0, arr.shape[0] // tile)]


def _segment_index(start, first_tile, n_tiles, i):
    return jnp.clip(i - start, 0, n_tiles - 1) + first_tile, 0


def _segment_specs(segments, tile, total_tiles):
    specs, start = [], 0
    for arr, first_tile, n_tiles in segments:
        specs.append(pl.BlockSpec((tile, arr.shape[1]),
                                  functools.partial(_segment_index, start, first_tile, n_tiles)))
        start += n_tiles
    assert start == total_tiles and len(segments) <= 2
    return specs


def _segment_rows(refs, n_first, rows):
    if len(refs) == 1:
        return refs[0][rows, :]
    return jnp.where(pl.program_id(0) < n_first, refs[0][rows, :], refs[1][rows, :])


def _sigmoid(x):
    return 1.0 / (1.0 + jnp.exp(-x))


def _rms_rows(x, g):
    ms = jnp.mean(x * x, axis=-1, keepdims=True)
    return x * lax.rsqrt(ms + EPS) * g


def _dot(a, b):
    return jnp.dot(a, b, preferred_element_type=F32)


def _dot_nt(a, b):
    return lax.dot_general(a, b, (((1,), (1,)), ((), ())), preferred_element_type=F32)


def _dot_tn(a, b):
    return lax.dot_general(a, b, (((0,), (0,)), ((), ())), preferred_element_type=F32)


def _split_bf16(x):
    hi = x.astype(BF16)
    lo = (x - hi.astype(F32)).astype(BF16)
    return hi, lo


def _head_rows(head, n):
    return pl.ds(head, n, stride=N_HEADS)


def _ffn_body(seg_counts, n_first, has_next, *refs):
    pos = seg_counts[0]
    x_refs = refs[:pos]
    pre = []
    for count in seg_counts[1:]:
        pre.append((refs[pos:pos + count], refs[pos + count]))
        pos += count + 1
    g_ref, wg_ref, wu_ref, wd_ref = refs[pos:pos + 4]
    pos += 4
    g2_ref = refs[pos] if has_next else None
    pos += int(has_next)
    xo_ref = refs[pos]
    ho_ref = refs[pos + 1] if has_next else None

    tile = xo_ref.shape[0]
    parts = [slice(k * (tile // FFN_ROW_GROUPS), (k + 1) * (tile // FFN_ROW_GROUPS)) for k in range(FFN_ROW_GROUPS)]
    xs = [_segment_rows(x_refs, n_first, r) for r in parts]
    for m_refs, w_ref in pre:
        xs = [x + _dot(_segment_rows(m_refs, n_first, r), w_ref[...]) for x, r in zip(xs, parts)]
    hs = [_rms_rows(x, g_ref[...]).astype(BF16) for x in xs]
    gates = [_dot(h, wg_ref[...]) for h in hs]
    ups = [_dot(h, wu_ref[...]) for h in hs]
    acts = [(a * _sigmoid(a) * b).astype(BF16) for a, b in zip(gates, ups)]
    downs = [_dot(act, wd_ref[...]) for act in acts]
    for x, y, r in zip(xs, downs, parts):
        x = x + FFN_SCALE * y
        xo_ref[r, :] = x
        if has_next:
            ho_ref[r, :] = _rms_rows(x, g2_ref[...]).astype(BF16)


def _ffn(x, pre, g, wg, wu, wd, g_next, tile):
    n_tiles = sum(seg[2] for seg in x)
    d = x[0][0].shape[1]
    has_next = g_next is not None
    operands = [x] + [m for m, _ in pre]
    n_first = max(segs[0][2] if len(segs) == 2 else 0 for segs in operands)
    assert all(len(segs) == 1 or segs[0][2] == n_first for segs in operands)
    in_specs = _segment_specs(x, tile, n_tiles)
    args = [seg[0] for seg in x]
    for m, (w, w_spec) in pre:
        in_specs += _segment_specs(m, tile, n_tiles) + [w_spec]
        args += [seg[0] for seg in m] + [w]
    for arr, spec in (g, wg, wu, wd):
        in_specs.append(spec)
        args.append(arr)
    out_shape = [jax.ShapeDtypeStruct((n_tiles * tile, d), F32)]
    out_specs = [_rows(tile, d)]
    if has_next:
        in_specs.append(g_next[1])
        args.append(g_next[0])
        out_shape.append(jax.ShapeDtypeStruct((n_tiles * tile, d), BF16))
        out_specs.append(_rows(tile, d))
    outs = pl.pallas_call(
        functools.partial(_ffn_body, tuple(len(segs) for segs in operands), n_first, has_next),
        out_shape=out_shape, grid=(n_tiles,), in_specs=in_specs, out_specs=out_specs,
        compiler_params=_params("parallel"), name="ffn")(*args)
    return (outs[0], outs[1]) if has_next else (outs[0], None)


def _abin_body(layer, h_ref, w_ref, lbp_ref, gq_ref, gk_ref,
               qa_ref, lf_ref, va_ref, ga_ref, qb_ref, kb_ref, vb_ref, k_ref, v_ref):
    tile, d = h_ref.shape
    group = tile // AB_ROW_GROUPS

    lbp = lbp_ref[...]
    e = jnp.exp(lbp - jnp.max(lbp, axis=0, keepdims=True))
    lb = jnp.sum(e[:layer + 1], axis=0, keepdims=True) / jnp.sum(e, axis=0, keepdims=True)

    for k in range(AB_ROW_GROUPS):
        r = slice(k * group, (k + 1) * group)
        h = h_ref[r, :]

        def seg(i, h=h):
            return _dot(h, w_ref[:, i * d:(i + 1) * d])

        qa = seg(0)
        qa_ref[r, :] = (qa * _sigmoid(qa)).astype(BF16)
        f = lb + (1.0 - lb) * _sigmoid(seg(1))
        lf_ref[r, :] = jnp.log(f)
        va_ref[r, :] = seg(2).astype(BF16)
        ga_ref[r, :] = _sigmoid(seg(3)).astype(BF16)

        qb = seg(4)
        kb = seg(5)
        vb = seg(6)
        vb_ref[r, :] = vb.astype(BF16)
        for hh in range(N_HEADS):
            sl = slice(hh * HEAD_DIM, (hh + 1) * HEAD_DIM)
            qb_ref[r, sl] = (_rms_rows(qb[:, sl], gq_ref[...]) * HEAD_DIM ** -0.5).astype(BF16)
            kn = _rms_rows(kb[:, sl], gk_ref[...])
            kb_ref[r, sl] = kn.astype(BF16)
            head_rows = pl.ds(k * group * N_HEADS + hh, group, stride=N_HEADS)
            k_ref[head_rows, :] = kn
            v_ref[head_rows, :] = vb[:, sl]


def _abin(h, first_tile, n, w_in, lb_param, g_q, g_k, layer, tile):
    d = h.shape[1]
    assert d == N_HEADS * HEAD_DIM
    feat = [BF16, F32, BF16, BF16, BF16, BF16, BF16]
    out_shape = [jax.ShapeDtypeStruct((n, d), t) for t in feat]
    out_specs = [_rows(tile, d) for _ in feat]
    for _ in range(2):
        out_shape.append(jax.ShapeDtypeStruct((n * N_HEADS, HEAD_DIM), F32))
        out_specs.append(_rows(tile * N_HEADS, HEAD_DIM))
    return pl.pallas_call(
        functools.partial(_abin_body, layer),
        out_shape=out_shape, grid=(n // tile,),
        in_specs=[_rows(tile, d, first_tile), w_in[1], _resident(lb_param.shape), g_q[1], g_k[1]],
        out_specs=out_specs,
        compiler_params=_params("parallel"), name="ab_in_proj")(
            h, w_in[0], lb_param, g_q[0], g_k[0])


def _hgrn_scores_any_decay(q, k, lf2, chunk):
    row = lax.broadcasted_iota(jnp.int32, (chunk, chunk), 0)
    col = lax.broadcasted_iota(jnp.int32, (chunk, chunk), 1)
    t1 = lax.broadcasted_iota(jnp.int32, (chunk, 1), 0)
    scores = jnp.where(row == col, jnp.sum(q * k, axis=-1, keepdims=True), 0.0)
    half = chunk // 2
    while half >= 1:
        blk = 2 * half
        same = (row // blk) == (col // blk)
        mid = (row // blk) * blk + half
        right = row >= mid
        seg = same & ((right & (col >= mid) & (col <= row)) | ((col > row) & (col < mid)))
        d2 = _dot(jnp.where(seg, 1.0, 0.0).astype(BF16), lf2)
        decay = jnp.exp(d2[:, :HEAD_DIM] + d2[:, HEAD_DIM:])
        right1 = (t1 % blk) >= half
        qe = jnp.where(right1, q * decay, 0.0).astype(BF16)
        ke = jnp.where(right1, 0.0, k * decay).astype(BF16)
        scores = scores + jnp.where(same, _dot_nt(qe, ke), 0.0)
        half //= 2
    return scores


def _hgrn_body(chunk, n_chunks, heads, has_s0, factored, *refs):
    q_ref, lf_ref, v_ref, gt_ref = refs[:4]
    pos = 4
    s0_ref = refs[pos] if has_s0 else None
    pos += int(has_s0)
    go_ref, o_ref, so_ref = refs[pos:pos + 3]

    row = lax.broadcasted_iota(jnp.int32, (chunk, chunk), 0)
    col = lax.broadcasted_iota(jnp.int32, (chunk, chunk), 1)
    causal = col <= row
    lower = jnp.where(causal, 1.0, 0.0).astype(BF16)
    mid = chunk // 2
    head_cols = [slice(j * HEAD_DIM, (j + 1) * HEAD_DIM) for j in range(heads)]

    def step(i, states):
        rows = pl.ds(pl.multiple_of(i * chunk, chunk), chunk)
        lfs = [lf_ref[rows, cl] for cl in head_cols]
        lf2s = [jnp.concatenate(_split_bf16(lf), axis=1) for lf in lfs]
        c2s = [_dot(lower, lf2) for lf2 in lf2s]
        qs = [q_ref[rows, cl].astype(F32) for cl in head_cols]
        ks = [1.0 - jnp.exp(lf) for lf in lfs]
        cs = [c2[:, :HEAD_DIM] + c2[:, HEAD_DIM:] for c2 in c2s]
        c_lasts = [c[chunk - 1:chunk, :] for c in cs]
        vs = [v_ref[rows, cl] for cl in head_cols]
        if factored:
            c_mids = [c[mid - 1:mid, :] for c in cs]
            rel = [c - cm for c, cm in zip(cs, c_mids)]
            qes = [(q * jnp.exp(a)).astype(BF16) for q, a in zip(qs, rel)]
            kes = [(k * jnp.exp(-a)).astype(BF16) for k, a in zip(ks, rel)]
            rhs = [jnp.concatenate([ke, (st * jnp.exp(cm)).astype(BF16)], axis=0)
                   for ke, st, cm in zip(kes, states, c_mids)]
            both = [_dot_nt(qe, r) for qe, r in zip(qes, rhs)]
            grown = [_dot_tn(v, ke) * jnp.exp(cl - cm) for v, ke, cl, cm in zip(vs, kes, c_lasts, c_mids)]
            scores = [jnp.where(causal, bt[:, :chunk], 0.0) for bt in both]
            inter = [bt[:, chunk:] for bt in both]
        else:
            inter = [_dot_nt((q * jnp.exp(c)).astype(BF16), st.astype(BF16)) for q, c, st in zip(qs, cs, states)]
            scores = [_hgrn_scores_any_decay(q, k, lf2, chunk) for q, k, lf2 in zip(qs, ks, lf2s)]
            kds = [(k * jnp.exp(cl - c)).astype(BF16) for k, cl, c in zip(ks, c_lasts, cs)]
            grown = [_dot_tn(v, kd) for v, kd in zip(vs, kds)]
        intra = [_dot(s.astype(BF16), v) for s, v in zip(scores, vs)]
        for j, cl in enumerate(head_cols):
            o = _rms_rows(inter[j] + intra[j], go_ref[j]) * gt_ref[rows, cl].astype(F32)
            o_ref[rows, cl] = o.astype(o_ref.dtype)
        return tuple(st * jnp.exp(cl) + g for st, cl, g in zip(states, c_lasts, grown))

    if has_s0:
        states = tuple(s0_ref[j].T for j in range(heads))
    else:
        states = tuple(jnp.zeros((HEAD_DIM, HEAD_DIM), F32) for _ in range(heads))
    if n_chunks > 1:
        states = lax.fori_loop(0, n_chunks, step, states, unroll=min(n_chunks, HGRN_UNROLL))
    else:
        states = step(0, states)
    for j in range(heads):
        so_ref[j] = states[j].T


def _hgrn_call(tokens, s0, g_out, batch, seq, chunk, heads, factored):
    n, d = tokens[0].shape
    has_s0 = s0 is not None
    width = heads * HEAD_DIM
    tok = pl.BlockSpec((seq, width), lambda b, h: (b, h))
    state = pl.BlockSpec((None, heads, HEAD_DIM, HEAD_DIM), lambda b, h: (b, h, 0, 0))
    in_specs = [tok] * len(tokens)
    args = list(tokens)
    if has_s0:
        in_specs.append(state)
        args.append(s0)
    in_specs.append(pl.BlockSpec((heads, 1, HEAD_DIM), lambda b, h: (h, 0, 0)))
    args.append(g_out.reshape(N_HEADS, 1, HEAD_DIM))
    return pl.pallas_call(
        functools.partial(_hgrn_body, chunk, seq // chunk, heads, has_s0, factored),
        out_shape=[jax.ShapeDtypeStruct((n, d), BF16),
                   jax.ShapeDtypeStruct((batch, N_HEADS, HEAD_DIM, HEAD_DIM), F32)],
        grid=(batch, N_HEADS // heads), in_specs=in_specs, out_specs=[tok, state],
        compiler_params=_params("parallel", "parallel"), name="hgrn2")(*args)


def _hgrn(tokens, s0, g_out, lb_param, layer, batch, seq):
    chunk = min(seq, HGRN_CHUNK)
    heads = N_HEADS if seq < HGRN_CHUNK else HGRN_HEADS_PER_STEP
    call = functools.partial(_hgrn_call, tokens, s0, g_out, batch, seq, chunk, heads)
    lb = jnp.cumsum(jax.nn.softmax(lb_param.astype(F32), axis=0), axis=0)[layer]
    reach = (chunk // 2) * jnp.max(-jnp.log(lb))
    return lax.cond(reach <= HGRN_SAFE_EXPONENT, lambda: call(True), lambda: call(False))


def _sb_logits_many(items):
    zs = [_dot_nt(q, kb) for q, kb, _, _ in items]
    sps = [jnp.maximum(z, 0.0) + jnp.log(1.0 + jnp.exp(-jnp.abs(z))) for z in zs]
    costs = [sp if it[3] is None else jnp.where(it[3], sp, 0.0) for sp, it in zip(sps, items)]
    suffix = [_dot(c.astype(BF16), it[2]) for c, it in zip(costs, items)]
    return [(z - sp - sf, jnp.sum(c, axis=-1, keepdims=True)) for z, sp, sf, c in zip(zs, sps, suffix, costs)]


def _sb_accumulate(logits, cost, vb, mask, acc, spent):
    w = jnp.exp(logits - spent)
    if mask is not None:
        w = jnp.where(mask, w, 0.0)
    return acc + _dot(w.astype(BF16), vb), spent + cost


def _strict_upper(n):
    j = lax.broadcasted_iota(jnp.int32, (n, n), 0)
    k = lax.broadcasted_iota(jnp.int32, (n, n), 1)
    return jnp.where(j > k, 1.0, 0.0).astype(BF16)


def _sb_body(blk, n_q, heads, past_blk, n_past, *refs):
    q_ref, k_ref, v_ref = refs[:3]
    if n_past:
        kp_new_ref, vp_new_ref, kp_hbm, vp_hbm, o_ref, kbuf, vbuf, sem = refs[3:]
    else:
        o_ref = refs[3]

    t = lax.broadcasted_iota(jnp.int32, (blk, blk), 0)
    s = lax.broadcasted_iota(jnp.int32, (blk, blk), 1)
    diag_mask = s < t
    upper = _strict_upper(blk)
    upper_past = _strict_upper(past_blk) if n_past else None
    head_cols = [slice(j * HEAD_DIM, (j + 1) * HEAD_DIM) for j in range(heads)]

    def still_live(spent):
        return functools.reduce(jnp.minimum, [jnp.min(r) for r in spent]) < -SB_EXIT

    for qi in range(n_q):
        rows = pl.ds(qi * blk, blk)
        qs = [q_ref[rows, c] for c in head_cols]
        n_older = qi + n_past
        items = [(qs[j], k_ref[rows, c], upper, diag_mask) for j, c in enumerate(head_cols)]
        values = [(v_ref[rows, c], diag_mask) for c in head_cols]
        if qi > 0:
            prev = pl.ds((qi - 1) * blk, blk)
            items += [(qs[j], k_ref[prev, c], upper, None) for j, c in enumerate(head_cols)]
            values += [(v_ref[prev, c], None) for c in head_cols]
        elif n_past:
            items += [(qs[j], kp_new_ref[_head_rows(j, past_blk), :].astype(BF16), upper_past, None)
                      for j in range(heads)]
            values += [(vp_new_ref[_head_rows(j, past_blk), :].astype(BF16), None) for j in range(heads)]
        logits = _sb_logits_many(items)
        accs = [jnp.zeros((blk, HEAD_DIM), F32) for _ in range(heads)]
        runs = [jnp.zeros((blk, 1), F32) for _ in range(heads)]
        for n, ((lg, stay), (vb, mask)) in enumerate(zip(logits, values)):
            j = n % heads
            accs[j], runs[j] = _sb_accumulate(lg, stay, vb, mask, accs[j], runs[j])

        if n_older > 1:
            def cond(carry):
                return (carry[0] < n_older) & carry[1]

            def body(carry, qi=qi, qs=qs):
                idx = carry[0]
                accs, runs = list(carry[2]), list(carry[3])
                if n_past:
                    src = pl.ds(pl.multiple_of((n_past - 1 - idx) * past_blk * N_HEADS, past_blk * N_HEADS),
                                past_blk * N_HEADS)
                    b = pl.program_id(0)
                    copies = [pltpu.make_async_copy(kp_hbm.at[b, src, :], kbuf, sem.at[0]),
                              pltpu.make_async_copy(vp_hbm.at[b, src, :], vbuf, sem.at[1])]
                    for cp in copies:
                        cp.start()
                    for cp in copies:
                        cp.wait()
                if n_past:
                    items = [(qs[j], kbuf[_head_rows(j, past_blk), :].astype(BF16), upper_past, None)
                             for j in range(heads)]
                    vbs = [vbuf[_head_rows(j, past_blk), :].astype(BF16) for j in range(heads)]
                else:
                    kr = pl.ds(pl.multiple_of((qi - 1 - idx) * blk, blk), blk)
                    items = [(qs[j], k_ref[kr, c], upper, None) for j, c in enumerate(head_cols)]
                    vbs = [v_ref[kr, c] for c in head_cols]
                for j, (lg, stay) in enumerate(_sb_logits_many(items)):
                    accs[j], runs[j] = _sb_accumulate(lg, stay, vbs[j], None, accs[j], runs[j])
                return idx + 1, still_live(runs), tuple(accs), tuple(runs)

            _, _, accs, _ = lax.while_loop(cond, body, (jnp.int32(1), still_live(runs), tuple(accs), tuple(runs)))
        for j, c in enumerate(head_cols):
            o_ref[rows, c] = accs[j].astype(o_ref.dtype)


def _stick_breaking(qn, kb, vb, k_past, v_past, batch, seq):
    n, d = qn.shape
    blk = min(seq, SB_BLOCK)
    n_q = seq // blk
    heads = N_HEADS if seq < SB_BLOCK else SB_HEADS_PER_STEP
    tok = pl.BlockSpec((seq, heads * HEAD_DIM), lambda b, h: (b, h))
    in_specs = [tok, tok, tok]
    args = [qn, kb, vb]
    scratch = []
    past_blk = n_past = 0
    if k_past is not None:
        assert n_q == 1 and heads == N_HEADS, "cached keys are only supported for a single query block"
        past_len = k_past.shape[1]
        past_blk = min(past_len, SB_BLOCK)
        n_past = past_len // past_blk
        slab = past_blk * N_HEADS
        newest = pl.BlockSpec((None, slab, HEAD_DIM), lambda b, h: (b, n_past - 1, 0))
        hbm = pl.BlockSpec(memory_space=pl.ANY)
        in_specs += [newest, newest, hbm, hbm]
        kp = k_past.reshape(batch, past_len * N_HEADS, HEAD_DIM)
        vp = v_past.reshape(batch, past_len * N_HEADS, HEAD_DIM)
        args += [kp, vp, kp, vp]
        scratch = [pltpu.VMEM((slab, HEAD_DIM), F32), pltpu.VMEM((slab, HEAD_DIM), F32),
                   pltpu.SemaphoreType.DMA((2,))]
    return pl.pallas_call(
        functools.partial(_sb_body, blk, n_q, heads, past_blk, n_past),
        out_shape=jax.ShapeDtypeStruct((n, d), BF16),
        grid=(batch, N_HEADS // heads), in_specs=in_specs, out_specs=tok,
        scratch_shapes=scratch,
        compiler_params=_params("parallel", "parallel"), name="stick_breaking")(*args)


def _twice_gelu_tanh(x):
    t = jnp.tanh(x * (0.7978845608028654 + (0.7978845608028654 * 0.044715) * (x * x)))
    return x + x * t


def _gmlp_body(length, n_sub, emit_v, h_ref, w_ref, g_ref, b_ref, ws_ref, bs_ref, y_ref, *v_out):
    h = h_ref[...]
    d_c = w_ref.shape[1] // 2
    dg = d_c // GMLP_GROUPS
    i = lax.broadcasted_iota(jnp.int32, (length, length), 0)
    j = lax.broadcasted_iota(jnp.int32, (length, length), 1)
    mask = (j // CHUNK) <= (i // CHUNK)
    half_bias = 0.5 * bs_ref[...]

    vs = [_twice_gelu_tanh(_dot(h, w_ref[:, d_c + g * dg:d_c + (g + 1) * dg])) for g in range(GMLP_GROUPS)]
    total = functools.reduce(jnp.add, [jnp.sum(v, axis=-1, keepdims=True) for v in vs])
    total_sq = functools.reduce(jnp.add, [jnp.sum(v * v, axis=-1, keepdims=True) for v in vs])
    mean = total / d_c
    inv = lax.rsqrt(total_sq / d_c - mean * mean + 4.0 * EPS)
    for g in range(GMLP_GROUPS):
        cols = slice(g * dg, (g + 1) * dg)
        vn = (vs[g] - mean) * inv * g_ref[:, cols] + b_ref[:, cols]
        if emit_v:
            v_out[0][:, cols] = vn
        vn = vn.astype(BF16)
        u = _twice_gelu_tanh(_dot(h, w_ref[:, cols]))
        w = jnp.where(mask, 0.5 * ws_ref[g], 0.0).astype(BF16)
        for c in range(n_sub):
            rows = slice(c * length, (c + 1) * length)
            mixed = _dot(w, vn[rows]) + half_bias[:, g:g + 1]
            y_ref[rows, cols] = (u[rows] * mixed).astype(BF16)


def _gmlp(h, first_tile, n, w_in, ln_g, ln_b, w_s, b_s, length, tile, emit_v):
    d = h.shape[1]
    d_c = w_in[0].shape[-1] // 2
    bs = b_s.T
    out_shape = [jax.ShapeDtypeStruct((n, d_c), BF16)]
    out_specs = [_rows(tile, d_c)]
    if emit_v:
        out_shape.append(jax.ShapeDtypeStruct((n, d_c), F32))
        out_specs.append(_rows(tile, d_c))
    outs = pl.pallas_call(
        functools.partial(_gmlp_body, length, tile // length, emit_v),
        out_shape=out_shape, grid=(n // tile,),
        in_specs=[_rows(tile, d, first_tile), w_in[1], ln_g[1], ln_b[1], _resident(w_s.shape), _resident(bs.shape)],
        out_specs=out_specs,
        compiler_params=_params("parallel"), name="gmlp")(h, w_in[0], ln_g[0], ln_b[0], w_s, bs)
    return (outs[0], outs[1]) if emit_v else (outs[0], None)


def _trunk(streams, p):
    d = streams[0]['x'].shape[-1]
    tile = TOKEN_TILE
    depth = p['ffn1_norm'].shape[0]
    for st in streams:
        st['batch'], st['seq'], _ = st['x'].shape
        st['n'] = st['batch'] * st['seq']
        assert st['n'] % tile == 0
        st['new_k'], st['new_v'], st['new_s'], st['new_cv'] = [], [], [], []
    first_tiles = [sum(st['n'] for st in streams[:i]) // tile for i in range(len(streams))]

    def row(name, l):
        v = p[name]
        return _layer(v.reshape(v.shape[0], 1, -1), l)

    def together(arrays):
        return [(a, 0, a.shape[0] // tile) for a in arrays]

    x = together([st['x'].reshape(st['n'], d) for st in streams])
    pre = []
    for l in range(depth):
        xa, h = _ffn(x, pre, row('ffn1_norm', l), _layer(p['ffn1_w_gate'], l), _layer(p['ffn1_w_up'], l),
                     _layer(p['ffn1_w_down'], l), row('mix_norm', l), tile)
        x = _whole(xa, tile)
        if l % 2 == 0:
            e = l // 2
            mix_a, mix_b = [], []
            for st, first in zip(streams, first_tiles):
                *hgrn_in, qb, kb, vb, k, v = _abin(h, first, st['n'], _layer(p['ab_w_in'], e), p['ab_lb'],
                                                   row('ab_g_q', e), row('ab_g_k', e), e, tile)
                state = None if st['hgrn_state'] is None else st['hgrn_state'][e]
                ma, s_new = _hgrn(hgrn_in, state, p['ab_g_out'][e], p['ab_lb'], e, st['batch'], st['seq'])
                mix_a.append(ma)
                mix_b.append(_stick_breaking(qb, kb, vb, None if st['sb_k'] is None else st['sb_k'][e],
                                             None if st['sb_v'] is None else st['sb_v'][e], st['batch'], st['seq']))
                st['new_k'].append(k.reshape(st['batch'], st['seq'], N_HEADS, HEAD_DIM))
                st['new_v'].append(v.reshape(st['batch'], st['seq'], N_HEADS, HEAD_DIM))
                st['new_s'].append(s_new)
            pre = [(together(mix_a), _layer(p['ab_w_out'], e, 0, 2)), (together(mix_b), _layer(p['ab_w_out'], e, 1, 2))]
        else:
            o = l // 2
            ys = []
            for st, first in zip(streams, first_tiles):
                length = min(st['seq'], GMLP_LEN)
                assert st['n'] % GMLP_TOKEN_TILE == 0 and (first * tile) % GMLP_TOKEN_TILE == 0
                y, cv = _gmlp(h, first * tile // GMLP_TOKEN_TILE, st['n'], _layer(p['c_w_in'], o),
                              row('c_ln_g', o), row('c_ln_b', o), p['c_w_s'][o, :, :length, :length],
                              p['c_b_s'][o, :, :length], length, GMLP_TOKEN_TILE, st['want_cv'])
                ys.append(y)
                if st['want_cv']:
                    st['new_cv'].append(cv.reshape(st['batch'], st['seq'], -1))
            pre = [(together(ys), _layer(p['c_w_out'], o))]
        if l + 1 < depth:
            xa, _ = _ffn(x, pre, row('ffn2_norm', l), _layer(p['ffn2_w_gate'], l), _layer(p['ffn2_w_up'], l),
                         _layer(p['ffn2_w_down'], l), None, tile)
            x = _whole(xa, tile)
            pre = []
    l = depth - 1
    outs = []
    for i, (st, first) in enumerate(zip(streams, first_tiles)):
        xi = [(x[0][0], first, st['n'] // tile)]
        pre_i = [([m[i]], w) for m, w in pre]
        y, _ = _ffn(xi, pre_i, row('ffn2_norm', l), _layer(p['ffn2_w_gate'], l), _layer(p['ffn2_w_up'], l),
                    _layer(p['ffn2_w_down'], l), None, tile)
        outs.append(y.reshape(st['batch'], st['seq'], d))
    return outs


_MATMUL_WEIGHTS = ('ffn1_w_gate', 'ffn1_w_up', 'ffn1_w_down', 'ffn2_w_gate', 'ffn2_w_up',
                   'ffn2_w_down', 'ab_w_in', 'ab_w_out', 'c_w_in', 'c_w_out')


def kernel(x_prompt, x_sample, cache_sb_k, cache_sb_v, state_hgrn, ffn1_norm, ffn1_w_gate, ffn1_w_up, ffn1_w_down, mix_norm, ffn2_norm, ffn2_w_gate, ffn2_w_up, ffn2_w_down, ab_w_in, ab_lb, ab_g_out, ab_g_q, ab_g_k, ab_w_out, c_w_in, c_ln_g, c_ln_b, c_w_s, c_b_s, c_w_out):
    p = {'ffn1_norm': ffn1_norm, 'ffn1_w_gate': ffn1_w_gate, 'ffn1_w_up': ffn1_w_up, 'ffn1_w_down': ffn1_w_down,
         'mix_norm': mix_norm, 'ffn2_norm': ffn2_norm, 'ffn2_w_gate': ffn2_w_gate, 'ffn2_w_up': ffn2_w_up,
         'ffn2_w_down': ffn2_w_down, 'ab_w_in': ab_w_in, 'ab_lb': ab_lb, 'ab_g_out': ab_g_out, 'ab_g_q': ab_g_q,
         'ab_g_k': ab_g_k, 'ab_w_out': ab_w_out, 'c_w_in': c_w_in, 'c_ln_g': c_ln_g, 'c_ln_b': c_ln_b,
         'c_w_s': c_w_s, 'c_b_s': c_b_s, 'c_w_out': c_w_out}
    for name in _MATMUL_WEIGHTS:
        p[name] = p[name].astype(BF16)
    prompt = {'x': x_prompt, 'hgrn_state': None, 'sb_k': None, 'sb_v': None, 'want_cv': False}
    sample = {'x': x_sample, 'hgrn_state': state_hgrn, 'sb_k': cache_sb_k, 'sb_v': cache_sb_v, 'want_cv': True}
    y_prompt, y_sample = _trunk([prompt, sample], p)
    return (y_prompt, y_sample, jnp.stack(prompt['new_k']), jnp.stack(prompt['new_v']), jnp.stack(prompt['new_s']),
            jnp.stack(sample['new_k']), jnp.stack(sample['new_v']), jnp.stack(sample['new_s']),
            jnp.stack(sample['new_cv']))
```

```python
import functools

import jax
import jax.numpy as jnp
from jax import lax
from jax.experimental import pallas as pl
from jax.experimental.pallas import tpu as pltpu

F32 = jnp.float32
BF16 = jnp.bfloat16

EPS = 1e-6
FFN_SCALE = 0.5
HEAD_DIM = 128
N_HEADS = 8
CHUNK = 64
GMLP_LEN = 128
GMLP_GROUPS = 8
V7X_VMEM_LIMIT_BYTES = 56 * 1024 * 1024
TOKEN_TILE = 512
GMLP_TOKEN_TILE = 1024
FFN_ROW_GROUPS = 2
AB_ROW_GROUPS = 2
HGRN_CHUNK = 128
HGRN_HEADS_PER_STEP = 4
HGRN_UNROLL = 16
HGRN_SAFE_EXPONENT = 60.0
SB_BLOCK = 256
SB_HEADS_PER_STEP = 4
SB_EXIT = -105.0


def _params(*sem):
    return pltpu.CompilerParams(dimension_semantics=sem, vmem_limit_bytes=V7X_VMEM_LIMIT_BYTES)


def _resident(shape):
    zeros = (0,) * len(shape)
    return pl.BlockSpec(shape, lambda *_: zeros, pipeline_mode=pl.Buffered(1))


def _layer(stacked, layer, row_block=0, n_row_blocks=1):
    rows = stacked.shape[1] // n_row_blocks
    block = (None, rows) + stacked.shape[2:]
    index = (layer, row_block) + (0,) * (stacked.ndim - 2)
    return stacked, pl.BlockSpec(block, lambda *_: index, pipeline_mode=pl.Buffered(1))


def _rows(tile, width, first_tile=0):
    return pl.BlockSpec((tile, width), lambda i: (i + first_tile, 0))


def _whole(arr, tile):
    return [(arr, 0, arr.shape[0] // tile)]


def _segment_index(start, first_tile, n_tiles, i):
    return jnp.clip(i - start, 0, n_tiles - 1) + first_tile, 0


def _segment_specs(segments, tile, total_tiles):
    specs, start = [], 0
    for arr, first_tile, n_tiles in segments:
        specs.append(pl.BlockSpec((tile, arr.shape[1]),
                                  functools.partial(_segment_index, start, first_tile, n_tiles)))
        start += n_tiles
    assert start == total_tiles and len(segments) <= 2
    return specs


def _segment_rows(refs, n_first, rows):
    if len(refs) == 1:
        return refs[0][rows, :]
    return jnp.where(pl.program_id(0) < n_first, refs[0][rows, :], refs[1][rows, :])


def _sigmoid(x):
    return 1.0 / (1.0 + jnp.exp(-x))


def _rms_rows(x, g):
    ms = jnp.mean(x * x, axis=-1, keepdims=True)
    return x * lax.rsqrt(ms + EPS) * g


def _dot(a, b):
    return jnp.dot(a, b, preferred_element_type=F32)


def _dot_nt(a, b):
    return lax.dot_general(a, b, (((1,), (1,)), ((), ())), preferred_element_type=F32)


def _dot_tn(a, b):
    return lax.dot_general(a, b, (((0,), (0,)), ((), ())), preferred_element_type=F32)


def _split_bf16(x):
    hi = x.astype(BF16)
    lo = (x - hi.astype(F32)).astype(BF16)
    return hi, lo


def _head_rows(head, n):
    return pl.ds(head, n, stride=N_HEADS)


def _ffn_body(seg_counts, n_first, has_next, *refs):
    pos = seg_counts[0]
    x_refs = refs[:pos]
    pre = []
    for count in seg_counts[1:]:
        pre.append((refs[pos:pos + count], refs[pos + count]))
        pos += count + 1
    g_ref, wg_ref, wu_ref, wd_ref = refs[pos:pos + 4]
    pos += 4
    g2_ref = refs[pos] if has_next else None
    pos += int(has_next)
    xo_ref = refs[pos]
    ho_ref = refs[pos + 1] if has_next else None

    tile = xo_ref.shape[0]
    parts = [slice(k * (tile // FFN_ROW_GROUPS), (k + 1) * (tile // FFN_ROW_GROUPS)) for k in range(FFN_ROW_GROUPS)]
    xs = [_segment_rows(x_refs, n_first, r) for r in parts]
    for m_refs, w_ref in pre:
        xs = [x + _dot(_segment_rows(m_refs, n_first, r), w_ref[...]) for x, r in zip(xs, parts)]
    hs = [_rms_rows(x, g_ref[...]).astype(BF16) for x in xs]
    gates = [_dot(h, wg_ref[...]) for h in hs]
    ups = [_dot(h, wu_ref[...]) for h in hs]
    acts = [(a * _sigmoid(a) * b).astype(BF16) for a, b in zip(gates, ups)]
    downs = [_dot(act, wd_ref[...]) for act in acts]
    for x, y, r in zip(xs, downs, parts):
        x = x + FFN_SCALE * y
        xo_ref[r, :] = x
        if has_next:
            ho_ref[r, :] = _rms_rows(x, g2_ref[...]).astype(BF16)


def _ffn(x, pre, g, wg, wu, wd, g_next, tile):
    n_tiles = sum(seg[2] for seg in x)
    d = x[0][0].shape[1]
    has_next = g_next is not None
    operands = [x] + [m for m, _ in pre]
    n_first = max(segs[0][2] if len(segs) == 2 else 0 for segs in operands)
    assert all(len(segs) == 1 or segs[0][2] == n_first for segs in operands)
    in_specs = _segment_specs(x, tile, n_tiles)
    args = [seg[0] for seg in x]
    for m, (w, w_spec) in pre:
        in_specs += _segment_specs(m, tile, n_tiles) + [w_spec]
        args += [seg[0] for seg in m] + [w]
    for arr, spec in (g, wg, wu, wd):
        in_specs.append(spec)
        args.append(arr)
    out_shape = [jax.ShapeDtypeStruct((n_tiles * tile, d), F32)]
    out_specs = [_rows(tile, d)]
    if has_next:
        in_specs.append(g_next[1])
        args.append(g_next[0])
        out_shape.append(jax.ShapeDtypeStruct((n_tiles * tile, d), BF16))
        out_specs.append(_rows(tile, d))
    outs = pl.pallas_call(
        functools.partial(_ffn_body, tuple(len(segs) for segs in operands), n_first, has_next),
        out_shape=out_shape, grid=(n_tiles,), in_specs=in_specs, out_specs=out_specs,
        compiler_params=_params("parallel"), name="ffn")(*args)
    return (outs[0], outs[1]) if has_next else (outs[0], None)


def _abin_body(layer, h_ref, w_ref, lbp_ref, gq_ref, gk_ref,
               qa_ref, lf_ref, va_ref, ga_ref, qb_ref, kb_ref, vb_ref, k_ref, v_ref):
    tile, d = h_ref.shape
    group = tile // AB_ROW_GROUPS

    lbp = lbp_ref[...]
    e = jnp.exp(lbp - jnp.max(lbp, axis=0, keepdims=True))
    lb = jnp.sum(e[:layer + 1], axis=0, keepdims=True) / jnp.sum(e, axis=0, keepdims=True)

    for k in range(AB_ROW_GROUPS):
        r = slice(k * group, (k + 1) * group)
        h = h_ref[r, :]

        def seg(i, h=h):
            return _dot(h, w_ref[:, i * d:(i + 1) * d])

        qa = seg(0)
        qa_ref[r, :] = (qa * _sigmoid(qa)).astype(BF16)
        f = lb + (1.0 - lb) * _sigmoid(seg(1))
        lf_ref[r, :] = jnp.log(f)
        va_ref[r, :] = seg(2).astype(BF16)
        ga_ref[r, :] = _sigmoid(seg(3)).astype(BF16)

        qb = seg(4)
        kb = seg(5)
        vb = seg(6)
        vb_ref[r, :] = vb.astype(BF16)
        for hh in range(N_HEADS):
            sl = slice(hh * HEAD_DIM, (hh + 1) * HEAD_DIM)
            qb_ref[r, sl] = (_rms_rows(qb[:, sl], gq_ref[...]) * HEAD_DIM ** -0.5).astype(BF16)
            kn = _rms_rows(kb[:, sl], gk_ref[...])
            kb_ref[r, sl] = kn.astype(BF16)
            head_rows = pl.ds(k * group * N_HEADS + hh, group, stride=N_HEADS)
            k_ref[head_rows, :] = kn
            v_ref[head_rows, :] = vb[:, sl]


def _abin(h, first_tile, n, w_in, lb_param, g_q, g_k, layer, tile):
    d = h.shape[1]
    assert d == N_HEADS * HEAD_DIM
    feat = [BF16, F32, BF16, BF16, BF16, BF16, BF16]
    out_shape = [jax.ShapeDtypeStruct((n, d), t) for t in feat]
    out_specs = [_rows(tile, d) for _ in feat]
    for _ in range(2):
        out_shape.append(jax.ShapeDtypeStruct((n * N_HEADS, HEAD_DIM), F32))
        out_specs.append(_rows(tile * N_HEADS, HEAD_DIM))
    return pl.pallas_call(
        functools.partial(_abin_body, layer),
        out_shape=out_shape, grid=(n // tile,),
        in_specs=[_rows(tile, d, first_tile), w_in[1], _resident(lb_param.shape), g_q[1], g_k[1]],
        out_specs=out_specs,
        compiler_params=_params("parallel"), name="ab_in_proj")(
            h, w_in[0], lb_param, g_q[0], g_k[0])


def _hgrn_scores_any_decay(q, k, lf2, chunk):
    row = lax.broadcasted_iota(jnp.int32, (chunk, chunk), 0)
    col = lax.broadcasted_iota(jnp.int32, (chunk, chunk), 1)
    t1 = lax.broadcasted_iota(jnp.int32, (chunk, 1), 0)
    scores = jnp.where(row == col, jnp.sum(q * k, axis=-1, keepdims=True), 0.0)
    half = chunk // 2
    while half >= 1:
        blk = 2 * half
        same = (row // blk) == (col // blk)
        mid = (row // blk) * blk + half
        right = row >= mid
        seg = same & ((right & (col >= mid) & (col <= row)) | ((col > row) & (col < mid)))
        d2 = _dot(jnp.where(seg, 1.0, 0.0).astype(BF16), lf2)
        decay = jnp.exp(d2[:, :HEAD_DIM] + d2[:, HEAD_DIM:])
        right1 = (t1 % blk) >= half
        qe = jnp.where(right1, q * decay, 0.0).astype(BF16)
        ke = jnp.where(right1, 0.0, k * decay).astype(BF16)
        scores = scores + jnp.where(same, _dot_nt(qe, ke), 0.0)
        half //= 2
    return scores


def _hgrn_body(chunk, n_chunks, heads, has_s0, factored, *refs):
    q_ref, lf_ref, v_ref, gt_ref = refs[:4]
    pos = 4
    s0_ref = refs[pos] if has_s0 else None
    pos += int(has_s0)
    go_ref, o_ref, so_ref = refs[pos:pos + 3]

    row = lax.broadcasted_iota(jnp.int32, (chunk, chunk), 0)
    col = lax.broadcasted_iota(jnp.int32, (chunk, chunk), 1)
    causal = col <= row
    lower = jnp.where(causal, 1.0, 0.0).astype(BF16)
    mid = chunk // 2
    head_cols = [slice(j * HEAD_DIM, (j + 1) * HEAD_DIM) for j in range(heads)]

    def step(i, states):
        rows = pl.ds(pl.multiple_of(i * chunk, chunk), chunk)
        lfs = [lf_ref[rows, cl] for cl in head_cols]
        lf2s = [jnp.concatenate(_split_bf16(lf), axis=1) for lf in lfs]
        c2s = [_dot(lower, lf2) for lf2 in lf2s]
        qs = [q_ref[rows, cl].astype(F32) for cl in head_cols]
        ks = [1.0 - jnp.exp(lf) for lf in lfs]
        cs = [c2[:, :HEAD_DIM] + c2[:, HEAD_DIM:] for c2 in c2s]
        c_lasts = [c[chunk - 1:chunk, :] for c in cs]
        vs = [v_ref[rows, cl] for cl in head_cols]
        if factored:
            c_mids = [c[mid - 1:mid, :] for c in cs]
            rel = [c - cm for c, cm in zip(cs, c_mids)]
            qes = [(q * jnp.exp(a)).astype(BF16) for q, a in zip(qs, rel)]
            kes = [(k * jnp.exp(-a)).astype(BF16) for k, a in zip(ks, rel)]
            rhs = [jnp.concatenate([ke, (st * jnp.exp(cm)).astype(BF16)], axis=0)
                   for ke, st, cm in zip(kes, states, c_mids)]
            both = [_dot_nt(qe, r) for qe, r in zip(qes, rhs)]
            grown = [_dot_tn(v, ke) * jnp.exp(cl - cm) for v, ke, cl, cm in zip(vs, kes, c_lasts, c_mids)]
            scores = [jnp.where(causal, bt[:, :chunk], 0.0) for bt in both]
            inter = [bt[:, chunk:] for bt in both]
        else:
            inter = [_dot_nt((q * jnp.exp(c)).astype(BF16), st.astype(BF16)) for q, c, st in zip(qs, cs, states)]
            scores = [_hgrn_scores_any_decay(q, k, lf2, chunk) for q, k, lf2 in zip(qs, ks, lf2s)]
            kds = [(k * jnp.exp(cl - c)).astype(BF16) for k, cl, c in zip(ks, c_lasts, cs)]
            grown = [_dot_tn(v, kd) for v, kd in zip(vs, kds)]
        intra = [_dot(s.astype(BF16), v) for s, v in zip(scores, vs)]
        for j, cl in enumerate(head_cols):
            o = _rms_rows(inter[j] + intra[j], go_ref[j]) * gt_ref[rows, cl].astype(F32)
            o_ref[rows, cl] = o.astype(o_ref.dtype)
        return tuple(st * jnp.exp(cl) + g for st, cl, g in zip(states, c_lasts, grown))

    if has_s0:
        states = tuple(s0_ref[j].T for j in range(heads))
    else:
        states = tuple(jnp.zeros((HEAD_DIM, HEAD_DIM), F32) for _ in range(heads))
    if n_chunks > 1:
        states = lax.fori_loop(0, n_chunks, step, states, unroll=min(n_chunks, HGRN_UNROLL))
    else:
        states = step(0, states)
    for j in range(heads):
        so_ref[j] = states[j].T


def _hgrn_call(tokens, s0, g_out, batch, seq, chunk, heads, factored):
    n, d = tokens[0].shape
    has_s0 = s0 is not None
    width = heads * HEAD_DIM
    tok = pl.BlockSpec((seq, width), lambda b, h: (b, h))
    state = pl.BlockSpec((None, heads, HEAD_DIM, HEAD_DIM), lambda b, h: (b, h, 0, 0))
    in_specs = [tok] * len(tokens)
    args = list(tokens)
    if has_s0:
        in_specs.append(state)
        args.append(s0)
    in_specs.append(pl.BlockSpec((heads, 1, HEAD_DIM), lambda b, h: (h, 0, 0)))
    args.append(g_out.reshape(N_HEADS, 1, HEAD_DIM))
    return pl.pallas_call(
        functools.partial(_hgrn_body, chunk, seq // chunk, heads, has_s0, factored),
        out_shape=[jax.ShapeDtypeStruct((n, d), BF16),
                   jax.ShapeDtypeStruct((batch, N_HEADS, HEAD_DIM, HEAD_DIM), F32)],
        grid=(batch, N_HEADS // heads), in_specs=in_specs, out_specs=[tok, state],
        compiler_params=_params("parallel", "parallel"), name="hgrn2")(*args)


def _hgrn(tokens, s0, g_out, lb_param, layer, batch, seq):
    chunk = min(seq, HGRN_CHUNK)
    heads = N_HEADS if seq < HGRN_CHUNK else HGRN_HEADS_PER_STEP
    call = functools.partial(_hgrn_call, tokens, s0, g_out, batch, seq, chunk, heads)
    lb = jnp.cumsum(jax.nn.softmax(lb_param.astype(F32), axis=0), axis=0)[layer]
    reach = (chunk // 2) * jnp.max(-jnp.log(lb))
    return lax.cond(reach <= HGRN_SAFE_EXPONENT, lambda: call(True), lambda: call(False))


def _sb_logits_many(items):
    zs = [_dot_nt(q, kb) for q, kb, _, _ in items]
    sps = [jnp.maximum(z, 0.0) + jnp.log(1.0 + jnp.exp(-jnp.abs(z))) for z in zs]
    costs = [sp if it[3] is None else jnp.where(it[3], sp, 0.0) for sp, it in zip(sps, items)]
    suffix = [_dot(c.astype(BF16), it[2]) for c, it in zip(costs, items)]
    return [(z - sp - sf, jnp.sum(c, axis=-1, keepdims=True)) for z, sp, sf, c in zip(zs, sps, suffix, costs)]


def _sb_accumulate(logits, cost, vb, mask, acc, spent):
    w = jnp.exp(logits - spent)
    if mask is not None:
        w = jnp.where(mask, w, 0.0)
    return acc + _dot(w.astype(BF16), vb), spent + cost


def _strict_upper(n):
    j = lax.broadcasted_iota(jnp.int32, (n, n), 0)
    k = lax.broadcasted_iota(jnp.int32, (n, n), 1)
    return jnp.where(j > k, 1.0, 0.0).astype(BF16)


def _sb_body(blk, n_q, heads, past_blk, n_past, *refs):
    q_ref, k_ref, v_ref = refs[:3]
    if n_past:
        kp_new_ref, vp_new_ref, kp_hbm, vp_hbm, o_ref, kbuf, vbuf, sem = refs[3:]
    else:
        o_ref = refs[3]

    t = lax.broadcasted_iota(jnp.int32, (blk, blk), 0)
    s = lax.broadcasted_iota(jnp.int32, (blk, blk), 1)
    diag_mask = s < t
    upper = _strict_upper(blk)
    upper_past = _strict_upper(past_blk) if n_past else None
    head_cols = [slice(j * HEAD_DIM, (j + 1) * HEAD_DIM) for j in range(heads)]

    def still_live(spent):
        return functools.reduce(jnp.minimum, [jnp.min(r) for r in spent]) < -SB_EXIT

    for qi in range(n_q):
        rows = pl.ds(qi * blk, blk)
        qs = [q_ref[rows, c] for c in head_cols]
        n_older = qi + n_past
        items = [(qs[j], k_ref[rows, c], upper, diag_mask) for j, c in enumerate(head_cols)]
        values = [(v_ref[rows, c], diag_mask) for c in head_cols]
        if qi > 0:
            prev = pl.ds((qi - 1) * blk, blk)
            items += [(qs[j], k_ref[prev, c], upper, None) for j, c in enumerate(head_cols)]
            values += [(v_ref[prev, c], None) for c in head_cols]
        elif n_past:
            items += [(qs[j], kp_new_ref[_head_rows(j, past_blk), :].astype(BF16), upper_past, None)
                      for j in range(heads)]
            values += [(vp_new_ref[_head_rows(j, past_blk), :].astype(BF16), None) for j in range(heads)]
        logits = _sb_logits_many(items)
        accs = [jnp.zeros((blk, HEAD_DIM), F32) for _ in range(heads)]
        runs = [jnp.zeros((blk, 1), F32) for _ in range(heads)]
        for n, ((lg, stay), (vb, mask)) in enumerate(zip(logits, values)):
            j = n % heads
            accs[j], runs[j] = _sb_accumulate(lg, stay, vb, mask, accs[j], runs[j])

        if n_older > 1:
            def cond(carry):
                return (carry[0] < n_older) & carry[1]

            def body(carry, qi=qi, qs=qs):
                idx = carry[0]
                accs, runs = list(carry[2]), list(carry[3])
                if n_past:
                    src = pl.ds(pl.multiple_of((n_past - 1 - idx) * past_blk * N_HEADS, past_blk * N_HEADS),
                                past_blk * N_HEADS)
                    b = pl.program_id(0)
                    copies = [pltpu.make_async_copy(kp_hbm.at[b, src, :], kbuf, sem.at[0]),
                              pltpu.make_async_copy(vp_hbm.at[b, src, :], vbuf, sem.at[1])]
                    for cp in copies:
                        cp.start()
                    for cp in copies:
                        cp.wait()
                if n_past:
                    items = [(qs[j], kbuf[_head_rows(j, past_blk), :].astype(BF16), upper_past, None)
                             for j in range(heads)]
                    vbs = [vbuf[_head_rows(j, past_blk), :].astype(BF16) for j in range(heads)]
                else:
                    kr = pl.ds(pl.multiple_of((qi - 1 - idx) * blk, blk), blk)
                    items = [(qs[j], k_ref[kr, c], upper, None) for j, c in enumerate(head_cols)]
                    vbs = [v_ref[kr, c] for c in head_cols]
                for j, (lg, stay) in enumerate(_sb_logits_many(items)):
                    accs[j], runs[j] = _sb_accumulate(lg, stay, vbs[j], None, accs[j], runs[j])
                return idx + 1, still_live(runs), tuple(accs), tuple(runs)

            _, _, accs, _ = lax.while_loop(cond, body, (jnp.int32(1), still_live(runs), tuple(accs), tuple(runs)))
        for j, c in enumerate(head_cols):
            o_ref[rows, c] = accs[j].astype(o_ref.dtype)


def _stick_breaking(qn, kb, vb, k_past, v_past, batch, seq):
    n, d = qn.shape
    blk = min(seq, SB_BLOCK)
    n_q = seq // blk
    heads = N_HEADS if seq < SB_BLOCK else SB_HEADS_PER_STEP
    tok = pl.BlockSpec((seq, heads * HEAD_DIM), lambda b, h: (b, h))
    in_specs = [tok, tok, tok]
    args = [qn, kb, vb]
    scratch = []
    past_blk = n_past = 0
    if k_past is not None:
        assert n_q == 1 and heads == N_HEADS, "cached keys are only supported for a single query block"
        past_len = k_past.shape[1]
        past_blk = min(past_len, SB_BLOCK)
        n_past = past_len // past_blk
        slab = past_blk * N_HEADS
        newest = pl.BlockSpec((None, slab, HEAD_DIM), lambda b, h: (b, n_past - 1, 0))
        hbm = pl.BlockSpec(memory_space=pl.ANY)
        in_specs += [newest, newest, hbm, hbm]
        kp = k_past.reshape(batch, past_len * N_HEADS, HEAD_DIM)
        vp = v_past.reshape(batch, past_len * N_HEADS, HEAD_DIM)
        args += [kp, vp, kp, vp]
        scratch = [pltpu.VMEM((slab, HEAD_DIM), F32), pltpu.VMEM((slab, HEAD_DIM), F32),
                   pltpu.SemaphoreType.DMA((2,))]
    return pl.pallas_call(
        functools.partial(_sb_body, blk, n_q, heads, past_blk, n_past),
        out_shape=jax.ShapeDtypeStruct((n, d), BF16),
        grid=(batch, N_HEADS // heads), in_specs=in_specs, out_specs=tok,
        scratch_shapes=scratch,
        compiler_params=_params("parallel", "parallel"), name="stick_breaking")(*args)


def _twice_gelu_tanh(x):
    t = jnp.tanh(x * (0.7978845608028654 + (0.7978845608028654 * 0.044715) * (x * x)))
    return x + x * t


def _gmlp_body(length, n_sub, emit_v, h_ref, w_ref, g_ref, b_ref, ws_ref, bs_ref, y_ref, *v_out):
    h = h_ref[...]
    d_c = w_ref.shape[1] // 2
    dg = d_c // GMLP_GROUPS
    i = lax.broadcasted_iota(jnp.int32, (length, length), 0)
    j = lax.broadcasted_iota(jnp.int32, (length, length), 1)
    mask = (j // CHUNK) <= (i // CHUNK)
    half_bias = 0.5 * bs_ref[...]

    vs = [_twice_gelu_tanh(_dot(h, w_ref[:, d_c + g * dg:d_c + (g + 1) * dg])) for g in range(GMLP_GROUPS)]
    total = functools.reduce(jnp.add, [jnp.sum(v, axis=-1, keepdims=True) for v in vs])
    total_sq = functools.reduce(jnp.add, [jnp.sum(v * v, axis=-1, keepdims=True) for v in vs])
    mean = total / d_c
    inv = lax.rsqrt(total_sq / d_c - mean * mean + 4.0 * EPS)
    for g in range(GMLP_GROUPS):
        cols = slice(g * dg, (g + 1) * dg)
        vn = (vs[g] - mean) * inv * g_ref[:, cols] + b_ref[:, cols]
        if emit_v:
            v_out[0][:, cols] = vn
        vn = vn.astype(BF16)
        u = _twice_gelu_tanh(_dot(h, w_ref[:, cols]))
        w = jnp.where(mask, 0.5 * ws_ref[g], 0.0).astype(BF16)
        for c in range(n_sub):
            rows = slice(c * length, (c + 1) * length)
            mixed = _dot(w, vn[rows]) + half_bias[:, g:g + 1]
            y_ref[rows, cols] = (u[rows] * mixed).astype(BF16)


def _gmlp(h, first_tile, n, w_in, ln_g, ln_b, w_s, b_s, length, tile, emit_v):
    d = h.shape[1]
    d_c = w_in[0].shape[-1] // 2
    bs = b_s.T
    out_shape = [jax.ShapeDtypeStruct((n, d_c), BF16)]
    out_specs = [_rows(tile, d_c)]
    if emit_v:
        out_shape.append(jax.ShapeDtypeStruct((n, d_c), F32))
        out_specs.append(_rows(tile, d_c))
    outs = pl.pallas_call(
        functools.partial(_gmlp_body, length, tile // length, emit_v),
        out_shape=out_shape, grid=(n // tile,),
        in_specs=[_rows(tile, d, first_tile), w_in[1], ln_g[1], ln_b[1], _resident(w_s.shape), _resident(bs.shape)],
        out_specs=out_specs,
        compiler_params=_params("parallel"), name="gmlp")(h, w_in[0], ln_g[0], ln_b[0], w_s, bs)
    return (outs[0], outs[1]) if emit_v else (outs[0], None)


def _trunk(streams, p):
    d = streams[0]['x'].shape[-1]
    tile = TOKEN_TILE
    depth = p['ffn1_norm'].shape[0]
    for st in streams:
        st['batch'], st['seq'], _ = st['x'].shape
        st['n'] = st['batch'] * st['seq']
        assert st['n'] % tile == 0
        st['new_k'], st['new_v'], st['new_s'], st['new_cv'] = [], [], [], []
    first_tiles = [sum(st['n'] for st in streams[:i]) // tile for i in range(len(streams))]

    def row(name, l):
        v = p[name]
        return _layer(v.reshape(v.shape[0], 1, -1), l)

    def together(arrays):
        return [(a, 0, a.shape[0] // tile) for a in arrays]

    x = together([st['x'].reshape(st['n'], d) for st in streams])
    pre = []
    for l in range(depth):
        xa, h = _ffn(x, pre, row('ffn1_norm', l), _layer(p['ffn1_w_gate'], l), _layer(p['ffn1_w_up'], l),
                     _layer(p['ffn1_w_down'], l), row('mix_norm', l), tile)
        x = _whole(xa, tile)
        if l % 2 == 0:
            e = l // 2
            mix_a, mix_b = [], []
            for st, first in zip(streams, first_tiles):
                *hgrn_in, qb, kb, vb, k, v = _abin(h, first, st['n'], _layer(p['ab_w_in'], e), p['ab_lb'],
                                                   row('ab_g_q', e), row('ab_g_k', e), e, tile)
                state = None if st['hgrn_state'] is None else st['hgrn_state'][e]
                ma, s_new = _hgrn(hgrn_in, state, p['ab_g_out'][e], p['ab_lb'], e, st['batch'], st['seq'])
                mix_a.append(ma)
                mix_b.append(_stick_breaking(qb, kb, vb, None if st['sb_k'] is None else st['sb_k'][e],
                                             None if st['sb_v'] is None else st['sb_v'][e], st['batch'], st['seq']))
                st['new_k'].append(k.reshape(st['batch'], st['seq'], N_HEADS, HEAD_DIM))
                st['new_v'].append(v.reshape(st['batch'], st['seq'], N_HEADS, HEAD_DIM))
                st['new_s'].append(s_new)
            pre = [(together(mix_a), _layer(p['ab_w_out'], e, 0, 2)), (together(mix_b), _layer(p['ab_w_out'], e, 1, 2))]
        else:
            o = l // 2
            ys = []
            for st, first in zip(streams, first_tiles):
                length = min(st['seq'], GMLP_LEN)
                assert st['n'] % GMLP_TOKEN_TILE == 0 and (first * tile) % GMLP_TOKEN_TILE == 0
                y, cv = _gmlp(h, first * tile // GMLP_TOKEN_TILE, st['n'], _layer(p['c_w_in'], o),
                              row('c_ln_g', o), row('c_ln_b', o), p['c_w_s'][o, :, :length, :length],
                              p['c_b_s'][o, :, :length], length, GMLP_TOKEN_TILE, st['want_cv'])
                ys.append(y)
                if st['want_cv']:
                    st['new_cv'].append(cv.reshape(st['batch'], st['seq'], -1))
            pre = [(together(ys), _layer(p['c_w_out'], o))]
        if l + 1 < depth:
            xa, _ = _ffn(x, pre, row('ffn2_norm', l), _layer(p['ffn2_w_gate'], l), _layer(p['ffn2_w_up'], l),
                         _layer(p['ffn2_w_down'], l), None, tile)
            x = _whole(xa, tile)
            pre = []
    l = depth - 1
    outs = []
    for i, (st, first) in enumerate(zip(streams, first_tiles)):
        xi = [(x[0][0], first, st['n'] // tile)]
        pre_i = [([m[i]], w) for m, w in pre]
        y, _ = _ffn(xi, pre_i, row('ffn2_norm', l), _layer(p['ffn2_w_gate'], l), _layer(p['ffn2_w_up'], l),
                    _layer(p['ffn2_w_down'], l), None, tile)
        outs.append(y.reshape(st['batch'], st['seq'], d))
    return outs


_MATMUL_WEIGHTS = ('ffn1_w_gate', 'ffn1_w_up', 'ffn1_w_down', 'ffn2_w_gate', 'ffn2_w_up',
                   'ffn2_w_down', 'ab_w_in', 'ab_w_out', 'c_w_in', 'c_w_out')


def kernel(x_prompt, x_sample, cache_sb_k, cache_sb_v, state_hgrn, ffn1_norm, ffn1_w_gate, ffn1_w_up, ffn1_w_down, mix_norm, ffn2_norm, ffn2_w_gate, ffn2_w_up, ffn2_w_down, ab_w_in, ab_lb, ab_g_out, ab_g_q, ab_g_k, ab_w_out, c_w_in, c_ln_g, c_ln_b, c_w_s, c_b_s, c_w_out):
    p = {'ffn1_norm': ffn1_norm, 'ffn1_w_gate': ffn1_w_gate, 'ffn1_w_up': ffn1_w_up, 'ffn1_w_down': ffn1_w_down,
         'mix_norm': mix_norm, 'ffn2_norm': ffn2_norm, 'ffn2_w_gate': ffn2_w_gate, 'ffn2_w_up': ffn2_w_up,
         'ffn2_w_down': ffn2_w_down, 'ab_w_in': ab_w_in, 'ab_lb': ab_lb, 'ab_g_out': ab_g_out, 'ab_g_q': ab_g_q,
         'ab_g_k': ab_g_k, 'ab_w_out': ab_w_out, 'c_w_in': c_w_in, 'c_ln_g': c_ln_g, 'c_ln_b': c_ln_b,
         'c_w_s': c_w_s, 'c_b_s': c_b_s, 'c_w_out': c_w_out}
    for name in _MATMUL_WEIGHTS:
        p[name] = p[name].astype(BF16)
    prompt = {'x': x_prompt, 'hgrn_state': None, 'sb_k': None, 'sb_v': None, 'want_cv': False}
    sample = {'x': x_sample, 'hgrn_state': state_hgrn, 'sb_k': cache_sb_k, 'sb_v': cache_sb_v, 'want_cv': True}
    y_prompt, y_sample = _trunk([prompt, sample], p)
    return (y_prompt, y_sample, jnp.stack(prompt['new_k']), jnp.stack(prompt['new_v']), jnp.stack(prompt['new_s']),
            jnp.stack(sample['new_k']), jnp.stack(sample['new_v']), jnp.stack(sample['new_s']),
            jnp.stack(sample['new_cv']))
```

```python
import functools

import jax
import jax.numpy as jnp
from jax import lax
from jax.experimental import pallas as pl
from jax.experimental.pallas import tpu as pltpu

F32 = jnp.float32
BF16 = jnp.bfloat16

EPS = 1e-6
FFN_SCALE = 0.5
HEAD_DIM = 128
N_HEADS = 8
CHUNK = 64
GMLP_LEN = 128
GMLP_GROUPS = 8
V7X_VMEM_LIMIT_BYTES = 56 * 1024 * 1024
TOKEN_TILE = 512
GMLP_TOKEN_TILE = 1024
FFN_ROW_GROUPS = 2
AB_ROW_GROUPS = 2
HGRN_CHUNK = 128
HGRN_HEADS_PER_STEP = 4
HGRN_UNROLL = 8
HGRN_SAFE_EXPONENT = 60.0
SB_BLOCK = 256
SB_HEADS_PER_STEP = 4
SB_EXIT = -105.0


def _params(*sem):
    return pltpu.CompilerParams(dimension_semantics=sem, vmem_limit_bytes=V7X_VMEM_LIMIT_BYTES)


def _resident(shape):
    zeros = (0,) * len(shape)
    return pl.BlockSpec(shape, lambda *_: zeros, pipeline_mode=pl.Buffered(1))


def _layer(stacked, layer, row_block=0, n_row_blocks=1):
    rows = stacked.shape[1] // n_row_blocks
    block = (None, rows) + stacked.shape[2:]
    index = (layer, row_block) + (0,) * (stacked.ndim - 2)
    return stacked, pl.BlockSpec(block, lambda *_: index, pipeline_mode=pl.Buffered(1))


def _rows(tile, width, first_tile=0):
    return pl.BlockSpec((tile, width), lambda i: (i + first_tile, 0))


def _whole(arr, tile):
    return [(arr, 0, arr.shape[0] // tile)]


def _segment_index(start, first_tile, n_tiles, i):
    return jnp.clip(i - start, 0, n_tiles - 1) + first_tile, 0


def _segment_specs(segments, tile, total_tiles):
    specs, start = [], 0
    for arr, first_tile, n_tiles in segments:
        specs.append(pl.BlockSpec((tile, arr.shape[1]),
                                  functools.partial(_segment_index, start, first_tile, n_tiles)))
        start += n_tiles
    assert start == total_tiles and len(segments) <= 2
    return specs


def _segment_rows(refs, n_first, rows):
    if len(refs) == 1:
        return refs[0][rows, :]
    return jnp.where(pl.program_id(0) < n_first, refs[0][rows, :], refs[1][rows, :])


def _sigmoid(x):
    return 1.0 / (1.0 + jnp.exp(-x))


def _rms_rows(x, g):
    ms = jnp.mean(x * x, axis=-1, keepdims=True)
    return x * lax.rsqrt(ms + EPS) * g


def _dot(a, b):
    return jnp.dot(a, b, preferred_element_type=F32)


def _dot_nt(a, b):
    return lax.dot_general(a, b, (((1,), (1,)), ((), ())), preferred_element_type=F32)


def _dot_tn(a, b):
    return lax.dot_general(a, b, (((0,), (0,)), ((), ())), preferred_element_type=F32)


def _split_bf16(x):
    hi = x.astype(BF16)
    lo = (x - hi.astype(F32)).astype(BF16)
    return hi, lo


def _head_rows(head, n):
    return pl.ds(head, n, stride=N_HEADS)


def _ffn_body(seg_counts, n_first, has_next, *refs):
    pos = seg_counts[0]
    x_refs = refs[:pos]
    pre = []
    for count in seg_counts[1:]:
        pre.append((refs[pos:pos + count], refs[pos + count]))
        pos += count + 1
    g_ref, wg_ref, wu_ref, wd_ref = refs[pos:pos + 4]
    pos += 4
    g2_ref = refs[pos] if has_next else None
    pos += int(has_next)
    xo_ref = refs[pos]
    ho_ref = refs[pos + 1] if has_next else None

    tile = xo_ref.shape[0]
    parts = [slice(k * (tile // FFN_ROW_GROUPS), (k + 1) * (tile // FFN_ROW_GROUPS)) for k in range(FFN_ROW_GROUPS)]
    xs = [_segment_rows(x_refs, n_first, r) for r in parts]
    for m_refs, w_ref in pre:
        xs = [x + _dot(_segment_rows(m_refs, n_first, r), w_ref[...]) for x, r in zip(xs, parts)]
    hs = [_rms_rows(x, g_ref[...]).astype(BF16) for x in xs]
    gates = [_dot(h, wg_ref[...]) for h in hs]
    ups = [_dot(h, wu_ref[...]) for h in hs]
    acts = [(a * _sigmoid(a) * b).astype(BF16) for a, b in zip(gates, ups)]
    downs = [_dot(act, wd_ref[...]) for act in acts]
    for x, y, r in zip(xs, downs, parts):
        x = x + FFN_SCALE * y
        xo_ref[r, :] = x
        if has_next:
            ho_ref[r, :] = _rms_rows(x, g2_ref[...]).astype(BF16)


def _ffn(x, pre, g, wg, wu, wd, g_next, tile):
    n_tiles = sum(seg[2] for seg in x)
    d = x[0][0].shape[1]
    has_next = g_next is not None
    operands = [x] + [m for m, _ in pre]
    n_first = max(segs[0][2] if len(segs) == 2 else 0 for segs in operands)
    assert all(len(segs) == 1 or segs[0][2] == n_first for segs in operands)
    in_specs = _segment_specs(x, tile, n_tiles)
    args = [seg[0] for seg in x]
    for m, (w, w_spec) in pre:
        in_specs += _segment_specs(m, tile, n_tiles) + [w_spec]
        args += [seg[0] for seg in m] + [w]
    for arr, spec in (g, wg, wu, wd):
        in_specs.append(spec)
        args.append(arr)
    out_shape = [jax.ShapeDtypeStruct((n_tiles * tile, d), F32)]
    out_specs = [_rows(tile, d)]
    if has_next:
        in_specs.append(g_next[1])
        args.append(g_next[0])
        out_shape.append(jax.ShapeDtypeStruct((n_tiles * tile, d), BF16))
        out_specs.append(_rows(tile, d))
    outs = pl.pallas_call(
        functools.partial(_ffn_body, tuple(len(segs) for segs in operands), n_first, has_next),
        out_shape=out_shape, grid=(n_tiles,), in_specs=in_specs, out_specs=out_specs,
        compiler_params=_params("parallel"), name="ffn")(*args)
    return (outs[0], outs[1]) if has_next else (outs[0], None)


def _abin_body(layer, h_ref, w_ref, lbp_ref, gq_ref, gk_ref,
               qa_ref, lf_ref, va_ref, ga_ref, qb_ref, kb_ref, vb_ref, k_ref, v_ref):
    tile, d = h_ref.shape
    group = tile // AB_ROW_GROUPS

    lbp = lbp_ref[...]
    e = jnp.exp(lbp - jnp.max(lbp, axis=0, keepdims=True))
    lb = jnp.sum(e[:layer + 1], axis=0, keepdims=True) / jnp.sum(e, axis=0, keepdims=True)

    for k in range(AB_ROW_GROUPS):
        r = slice(k * group, (k + 1) * group)
        h = h_ref[r, :]

        def seg(i, h=h):
            return _dot(h, w_ref[:, i * d:(i + 1) * d])

        qa = seg(0)
        qa_ref[r, :] = (qa * _sigmoid(qa)).astype(BF16)
        f = lb + (1.0 - lb) * _sigmoid(seg(1))
        lf_ref[r, :] = jnp.log(f)
        va_ref[r, :] = seg(2).astype(BF16)
        ga_ref[r, :] = _sigmoid(seg(3)).astype(BF16)

        qb = seg(4)
        kb = seg(5)
        vb = seg(6)
        vb_ref[r, :] = vb.astype(BF16)
        for hh in range(N_HEADS):
            sl = slice(hh * HEAD_DIM, (hh + 1) * HEAD_DIM)
            qb_ref[r, sl] = (_rms_rows(qb[:, sl], gq_ref[...]) * HEAD_DIM ** -0.5).astype(BF16)
            kn = _rms_rows(kb[:, sl], gk_ref[...])
            kb_ref[r, sl] = kn.astype(BF16)
            head_rows = pl.ds(k * group * N_HEADS + hh, group, stride=N_HEADS)
            k_ref[head_rows, :] = kn
            v_ref[head_rows, :] = vb[:, sl]


def _abin(h, first_tile, n, w_in, lb_param, g_q, g_k, layer, tile):
    d = h.shape[1]
    assert d == N_HEADS * HEAD_DIM
    feat = [BF16, F32, BF16, BF16, BF16, BF16, BF16]
    out_shape = [jax.ShapeDtypeStruct((n, d), t) for t in feat]
    out_specs = [_rows(tile, d) for _ in feat]
    for _ in range(2):
        out_shape.append(jax.ShapeDtypeStruct((n * N_HEADS, HEAD_DIM), F32))
        out_specs.append(_rows(tile * N_HEADS, HEAD_DIM))
    return pl.pallas_call(
        functools.partial(_abin_body, layer),
        out_shape=out_shape, grid=(n // tile,),
        in_specs=[_rows(tile, d, first_tile), w_in[1], _resident(lb_param.shape), g_q[1], g_k[1]],
        out_specs=out_specs,
        compiler_params=_params("parallel"), name="ab_in_proj")(
            h, w_in[0], lb_param, g_q[0], g_k[0])


def _hgrn_scores_any_decay(q, k, lf2, chunk):
    row = lax.broadcasted_iota(jnp.int32, (chunk, chunk), 0)
    col = lax.broadcasted_iota(jnp.int32, (chunk, chunk), 1)
    t1 = lax.broadcasted_iota(jnp.int32, (chunk, 1), 0)
    scores = jnp.where(row == col, jnp.sum(q * k, axis=-1, keepdims=True), 0.0)
    half = chunk // 2
    while half >= 1:
        blk = 2 * half
        same = (row // blk) == (col // blk)
        mid = (row // blk) * blk + half
        right = row >= mid
        seg = same & ((right & (col >= mid) & (col <= row)) | ((col > row) & (col < mid)))
        d2 = _dot(jnp.where(seg, 1.0, 0.0).astype(BF16), lf2)
        decay = jnp.exp(d2[:, :HEAD_DIM] + d2[:, HEAD_DIM:])
        right1 = (t1 % blk) >= half
        qe = jnp.where(right1, q * decay, 0.0).astype(BF16)
        ke = jnp.where(right1, 0.0, k * decay).astype(BF16)
        scores = scores + jnp.where(same, _dot_nt(qe, ke), 0.0)
        half //= 2
    return scores


def _hgrn_body(chunk, n_chunks, heads, has_s0, factored, *refs):
    q_ref, lf_ref, v_ref, gt_ref = refs[:4]
    pos = 4
    s0_ref = refs[pos] if has_s0 else None
    pos += int(has_s0)
    go_ref, o_ref, so_ref = refs[pos:pos + 3]

    row = lax.broadcasted_iota(jnp.int32, (chunk, chunk), 0)
    col = lax.broadcasted_iota(jnp.int32, (chunk, chunk), 1)
    causal = col <= row
    lower = jnp.where(causal, 1.0, 0.0).astype(BF16)
    mid = chunk // 2
    head_cols = [slice(j * HEAD_DIM, (j + 1) * HEAD_DIM) for j in range(heads)]

    def step(i, states):
        rows = pl.ds(pl.multiple_of(i * chunk, chunk), chunk)
        lfs = [lf_ref[rows, cl] for cl in head_cols]
        lf2s = [jnp.concatenate(_split_bf16(lf), axis=1) for lf in lfs]
        c2s = [_dot(lower, lf2) for lf2 in lf2s]
        qs = [q_ref[rows, cl].astype(F32) for cl in head_cols]
        ks = [1.0 - jnp.exp(lf) for lf in lfs]
        cs = [c2[:, :HEAD_DIM] + c2[:, HEAD_DIM:] for c2 in c2s]
        c_lasts = [c[chunk - 1:chunk, :] for c in cs]
        vs = [v_ref[rows, cl] for cl in head_cols]
        if factored:
            c_mids = [c[mid - 1:mid, :] for c in cs]
            rel = [c - cm for c, cm in zip(cs, c_mids)]
            qes = [(q * jnp.exp(a)).astype(BF16) for q, a in zip(qs, rel)]
            kes = [(k * jnp.exp(-a)).astype(BF16) for k, a in zip(ks, rel)]
            rhs = [jnp.concatenate([ke, (st * jnp.exp(cm)).astype(BF16)], axis=0)
                   for ke, st, cm in zip(kes, states, c_mids)]
            both = [_dot_nt(qe, r) for qe, r in zip(qes, rhs)]
            grown = [_dot_tn(v, ke) * jnp.exp(cl - cm) for v, ke, cl, cm in zip(vs, kes, c_lasts, c_mids)]
            scores = [jnp.where(causal, bt[:, :chunk], 0.0) for bt in both]
            inter = [bt[:, chunk:] for bt in both]
        else:
            inter = [_dot_nt((q * jnp.exp(c)).astype(BF16), st.astype(BF16)) for q, c, st in zip(qs, cs, states)]
            scores = [_hgrn_scores_any_decay(q, k, lf2, chunk) for q, k, lf2 in zip(qs, ks, lf2s)]
            kds = [(k * jnp.exp(cl - c)).astype(BF16) for k, cl, c in zip(ks, c_lasts, cs)]
            grown = [_dot_tn(v, kd) for v, kd in zip(vs, kds)]
        intra = [_dot(s.astype(BF16), v) for s, v in zip(scores, vs)]
        for j, cl in enumerate(head_cols):
            o = _rms_rows(inter[j] + intra[j], go_ref[j]) * gt_ref[rows, cl].astype(F32)
            o_ref[rows, cl] = o.astype(o_ref.dtype)
        return tuple(st * jnp.exp(cl) + g for st, cl, g in zip(states, c_lasts, grown))

    if has_s0:
        states = tuple(s0_ref[j].T for j in range(heads))
    else:
        states = tuple(jnp.zeros((HEAD_DIM, HEAD_DIM), F32) for _ in range(heads))
    if n_chunks > 1:
        states = lax.fori_loop(0, n_chunks, step, states, unroll=min(n_chunks, HGRN_UNROLL))
    else:
        states = step(0, states)
    for j in range(heads):
        so_ref[j] = states[j].T


def _hgrn_plan(tokens, s0, g_out, batch, seq, chunk, heads, factored):
    n, d = tokens[0].shape
    has_s0 = s0 is not None
    width = heads * HEAD_DIM
    tok = pl.BlockSpec((seq, width), lambda b, h: (b, h))
    state = pl.BlockSpec((None, heads, HEAD_DIM, HEAD_DIM), lambda b, h: (b, h, 0, 0))
    in_specs = [tok] * len(tokens)
    args = list(tokens)
    if has_s0:
        in_specs.append(state)
        args.append(s0)
    in_specs.append(pl.BlockSpec((heads, 1, HEAD_DIM), lambda b, h: (h, 0, 0)))
    args.append(g_out.reshape(N_HEADS, 1, HEAD_DIM))
    return dict(body=functools.partial(_hgrn_body, chunk, seq // chunk, heads, has_s0, factored),
                out_shape=[jax.ShapeDtypeStruct((n, d), BF16),
                           jax.ShapeDtypeStruct((batch, N_HEADS, HEAD_DIM, HEAD_DIM), F32)],
                grid=(batch, N_HEADS // heads), in_specs=in_specs, out_specs=[tok, state], scratch=[], args=args)


def _run(plan, name):
    return pl.pallas_call(plan['body'], out_shape=plan['out_shape'], grid=plan['grid'], in_specs=plan['in_specs'],
                          out_specs=plan['out_specs'], scratch_shapes=plan['scratch'],
                          compiler_params=_params("parallel", "parallel"), name=name)(*plan['args'])


def _hgrn_shape(seq):
    return min(seq, HGRN_CHUNK), (N_HEADS if seq < HGRN_CHUNK else HGRN_HEADS_PER_STEP)


def _hgrn_can_factor(lb_param, layer, chunk):
    lb = jnp.cumsum(jax.nn.softmax(lb_param.astype(F32), axis=0), axis=0)[layer]
    return (chunk // 2) * jnp.max(-jnp.log(lb)) <= HGRN_SAFE_EXPONENT


def _hgrn(tokens, s0, g_out, lb_param, layer, batch, seq):
    chunk, heads = _hgrn_shape(seq)
    call = lambda factored: _run(_hgrn_plan(tokens, s0, g_out, batch, seq, chunk, heads, factored), "hgrn2")
    return lax.cond(_hgrn_can_factor(lb_param, layer, chunk), lambda: call(True), lambda: call(False))


def _sb_logits_many(items):
    zs = [_dot_nt(q, kb) for q, kb, _, _ in items]
    sps = [jnp.maximum(z, 0.0) + jnp.log(1.0 + jnp.exp(-jnp.abs(z))) for z in zs]
    costs = [sp if it[3] is None else jnp.where(it[3], sp, 0.0) for sp, it in zip(sps, items)]
    suffix = [_dot(c.astype(BF16), it[2]) for c, it in zip(costs, items)]
    return [(z - sp - sf, jnp.sum(c, axis=-1, keepdims=True)) for z, sp, sf, c in zip(zs, sps, suffix, costs)]


def _sb_accumulate(logits, cost, vb, mask, acc, spent):
    w = jnp.exp(logits - spent)
    if mask is not None:
        w = jnp.where(mask, w, 0.0)
    return acc + _dot(w.astype(BF16), vb), spent + cost


def _strict_upper(n):
    j = lax.broadcasted_iota(jnp.int32, (n, n), 0)
    k = lax.broadcasted_iota(jnp.int32, (n, n), 1)
    return jnp.where(j > k, 1.0, 0.0).astype(BF16)


def _sb_body(blk, n_q, heads, past_blk, n_past, *refs):
    q_ref, k_ref, v_ref = refs[:3]
    if n_past:
        kp_new_ref, vp_new_ref, kp_hbm, vp_hbm, o_ref, kbuf, vbuf, sem = refs[3:]
    else:
        o_ref = refs[3]

    t = lax.broadcasted_iota(jnp.int32, (blk, blk), 0)
    s = lax.broadcasted_iota(jnp.int32, (blk, blk), 1)
    diag_mask = s < t
    upper = _strict_upper(blk)
    upper_past = _strict_upper(past_blk) if n_past else None
    head_cols = [slice(j * HEAD_DIM, (j + 1) * HEAD_DIM) for j in range(heads)]

    def still_live(spent):
        return functools.reduce(jnp.minimum, [jnp.min(r) for r in spent]) < -SB_EXIT

    for qi in range(n_q):
        rows = pl.ds(qi * blk, blk)
        qs = [q_ref[rows, c] for c in head_cols]
        n_older = qi + n_past
        items = [(qs[j], k_ref[rows, c], upper, diag_mask) for j, c in enumerate(head_cols)]
        values = [(v_ref[rows, c], diag_mask) for c in head_cols]
        if qi > 0:
            prev = pl.ds((qi - 1) * blk, blk)
            items += [(qs[j], k_ref[prev, c], upper, None) for j, c in enumerate(head_cols)]
            values += [(v_ref[prev, c], None) for c in head_cols]
        elif n_past:
            items += [(qs[j], kp_new_ref[_head_rows(j, past_blk), :].astype(BF16), upper_past, None)
                      for j in range(heads)]
            values += [(vp_new_ref[_head_rows(j, past_blk), :].astype(BF16), None) for j in range(heads)]
        logits = _sb_logits_many(items)
        accs = [jnp.zeros((blk, HEAD_DIM), F32) for _ in range(heads)]
        runs = [jnp.zeros((blk, 1), F32) for _ in range(heads)]
        for n, ((lg, stay), (vb, mask)) in enumerate(zip(logits, values)):
            j = n % heads
            accs[j], runs[j] = _sb_accumulate(lg, stay, vb, mask, accs[j], runs[j])

        if n_older > 1:
            def cond(carry):
                return (carry[0] < n_older) & carry[1]

            def body(carry, qi=qi, qs=qs):
                idx = carry[0]
                accs, runs = list(carry[2]), list(carry[3])
                if n_past:
                    src = pl.ds(pl.multiple_of((n_past - 1 - idx) * past_blk * N_HEADS, past_blk * N_HEADS),
                                past_blk * N_HEADS)
                    b = pl.program_id(0)
                    copies = [pltpu.make_async_copy(kp_hbm.at[b, src, :], kbuf, sem.at[0]),
                              pltpu.make_async_copy(vp_hbm.at[b, src, :], vbuf, sem.at[1])]
                    for cp in copies:
                        cp.start()
                    for cp in copies:
                        cp.wait()
                if n_past:
                    items = [(qs[j], kbuf[_head_rows(j, past_blk), :].astype(BF16), upper_past, None)
                             for j in range(heads)]
                    vbs = [vbuf[_head_rows(j, past_blk), :].astype(BF16) for j in range(heads)]
                else:
                    kr = pl.ds(pl.multiple_of((qi - 1 - idx) * blk, blk), blk)
                    items = [(qs[j], k_ref[kr, c], upper, None) for j, c in enumerate(head_cols)]
                    vbs = [v_ref[kr, c] for c in head_cols]
                for j, (lg, stay) in enumerate(_sb_logits_many(items)):
                    accs[j], runs[j] = _sb_accumulate(lg, stay, vbs[j], None, accs[j], runs[j])
                return idx + 1, still_live(runs), tuple(accs), tuple(runs)

            _, _, accs, _ = lax.while_loop(cond, body, (jnp.int32(1), still_live(runs), tuple(accs), tuple(runs)))
        for j, c in enumerate(head_cols):
            o_ref[rows, c] = accs[j].astype(o_ref.dtype)


def _sb_plan(qn, kb, vb, k_past, v_past, batch, seq):
    n, d = qn.shape
    blk = min(seq, SB_BLOCK)
    n_q = seq // blk
    heads = N_HEADS if seq < SB_BLOCK else SB_HEADS_PER_STEP
    tok = pl.BlockSpec((seq, heads * HEAD_DIM), lambda b, h: (b, h))
    in_specs = [tok, tok, tok]
    args = [qn, kb, vb]
    scratch = []
    past_blk = n_past = 0
    if k_past is not None:
        assert n_q == 1 and heads == N_HEADS, "cached keys are only supported for a single query block"
        past_len = k_past.shape[1]
        past_blk = min(past_len, SB_BLOCK)
        n_past = past_len // past_blk
        slab = past_blk * N_HEADS
        newest = pl.BlockSpec((None, slab, HEAD_DIM), lambda b, h: (b, n_past - 1, 0))
        hbm = pl.BlockSpec(memory_space=pl.ANY)
        in_specs += [newest, newest, hbm, hbm]
        kp = k_past.reshape(batch, past_len * N_HEADS, HEAD_DIM)
        vp = v_past.reshape(batch, past_len * N_HEADS, HEAD_DIM)
        args += [kp, vp, kp, vp]
        scratch = [pltpu.VMEM((slab, HEAD_DIM), F32), pltpu.VMEM((slab, HEAD_DIM), F32),
                   pltpu.SemaphoreType.DMA((2,))]
    return dict(body=functools.partial(_sb_body, blk, n_q, heads, past_blk, n_past),
                out_shape=[jax.ShapeDtypeStruct((n, d), BF16)], grid=(batch, N_HEADS // heads),
                in_specs=in_specs, out_specs=[tok], scratch=scratch, args=args)


def _stick_breaking(qn, kb, vb, k_past, v_past, batch, seq):
    return _run(_sb_plan(qn, kb, vb, k_past, v_past, batch, seq), "stick_breaking")[0]


def _both_body(first, second, *refs):
    n_in = len(first['in_specs']) + len(second['in_specs'])
    n_out = len(first['out_specs']) + len(second['out_specs'])
    ins, outs, scratch = refs[:n_in], refs[n_in:n_in + n_out], refs[n_in + n_out:]
    first['body'](*ins[:len(first['in_specs'])], *outs[:len(first['out_specs'])])
    second['body'](*ins[len(first['in_specs']):], *outs[len(first['out_specs']):], *scratch)


def _short_sequence_mixers(tokens, s0, g_out, lb_param, layer, qn, kb, vb, k_past, v_past, batch, seq):
    chunk, heads = _hgrn_shape(seq)

    def call(factored):
        first = _hgrn_plan(tokens, s0, g_out, batch, seq, chunk, heads, factored)
        second = _sb_plan(qn, kb, vb, k_past, v_past, batch, seq)
        assert first['grid'] == second['grid'] and not first['scratch']
        both = dict(body=functools.partial(_both_body, first, second),
                    out_shape=first['out_shape'] + second['out_shape'], grid=first['grid'],
                    in_specs=first['in_specs'] + second['in_specs'],
                    out_specs=first['out_specs'] + second['out_specs'],
                    scratch=second['scratch'], args=first['args'] + second['args'])
        return tuple(_run(both, "short_sequence_mixers"))

    return lax.cond(_hgrn_can_factor(lb_param, layer, chunk), lambda: call(True), lambda: call(False))


def _twice_gelu_tanh(x):
    t = jnp.tanh(x * (0.7978845608028654 + (0.7978845608028654 * 0.044715) * (x * x)))
    return x + x * t


def _gmlp_body(length, n_sub, emit_v, h_ref, w_ref, g_ref, b_ref, ws_ref, bs_ref, y_ref, *v_out):
    h = h_ref[...]
    d_c = w_ref.shape[1] // 2
    dg = d_c // GMLP_GROUPS
    i = lax.broadcasted_iota(jnp.int32, (length, length), 0)
    j = lax.broadcasted_iota(jnp.int32, (length, length), 1)
    mask = (j // CHUNK) <= (i // CHUNK)
    half_bias = 0.5 * bs_ref[...]

    vs = [_twice_gelu_tanh(_dot(h, w_ref[:, d_c + g * dg:d_c + (g + 1) * dg])) for g in range(GMLP_GROUPS)]
    total = functools.reduce(jnp.add, [jnp.sum(v, axis=-1, keepdims=True) for v in vs])
    total_sq = functools.reduce(jnp.add, [jnp.sum(v * v, axis=-1, keepdims=True) for v in vs])
    mean = total / d_c
    inv = lax.rsqrt(total_sq / d_c - mean * mean + 4.0 * EPS)
    for g in range(GMLP_GROUPS):
        cols = slice(g * dg, (g + 1) * dg)
        vn = (vs[g] - mean) * inv * g_ref[:, cols] + b_ref[:, cols]
        if emit_v:
            v_out[0][:, cols] = vn
        vn = vn.astype(BF16)
        u = _twice_gelu_tanh(_dot(h, w_ref[:, cols]))
        w = jnp.where(mask, 0.5 * ws_ref[g], 0.0).astype(BF16)
        for c in range(n_sub):
            rows = slice(c * length, (c + 1) * length)
            mixed = _dot(w, vn[rows]) + half_bias[:, g:g + 1]
            y_ref[rows, cols] = (u[rows] * mixed).astype(BF16)


def _gmlp(h, first_tile, n, w_in, ln_g, ln_b, w_s, b_s, length, tile, emit_v):
    d = h.shape[1]
    d_c = w_in[0].shape[-1] // 2
    bs = b_s.T
    out_shape = [jax.ShapeDtypeStruct((n, d_c), BF16)]
    out_specs = [_rows(tile, d_c)]
    if emit_v:
        out_shape.append(jax.ShapeDtypeStruct((n, d_c), F32))
        out_specs.append(_rows(tile, d_c))
    outs = pl.pallas_call(
        functools.partial(_gmlp_body, length, tile // length, emit_v),
        out_shape=out_shape, grid=(n // tile,),
        in_specs=[_rows(tile, d, first_tile), w_in[1], ln_g[1], ln_b[1], _resident(w_s.shape), _resident(bs.shape)],
        out_specs=out_specs,
        compiler_params=_params("parallel"), name="gmlp")(h, w_in[0], ln_g[0], ln_b[0], w_s, bs)
    return (outs[0], outs[1]) if emit_v else (outs[0], None)


def _trunk(streams, p):
    d = streams[0]['x'].shape[-1]
    tile = TOKEN_TILE
    depth = p['ffn1_norm'].shape[0]
    for st in streams:
        st['batch'], st['seq'], _ = st['x'].shape
        st['n'] = st['batch'] * st['seq']
        assert st['n'] % tile == 0
        st['new_k'], st['new_v'], st['new_s'], st['new_cv'] = [], [], [], []
    first_tiles = [sum(st['n'] for st in streams[:i]) // tile for i in range(len(streams))]

    def row(name, l):
        v = p[name]
        return _layer(v.reshape(v.shape[0], 1, -1), l)

    def together(arrays):
        return [(a, 0, a.shape[0] // tile) for a in arrays]

    x = together([st['x'].reshape(st['n'], d) for st in streams])
    pre = []
    for l in range(depth):
        xa, h = _ffn(x, pre, row('ffn1_norm', l), _layer(p['ffn1_w_gate'], l), _layer(p['ffn1_w_up'], l),
                     _layer(p['ffn1_w_down'], l), row('mix_norm', l), tile)
        x = _whole(xa, tile)
        if l % 2 == 0:
            e = l // 2
            mix_a, mix_b = [], []
            for st, first in zip(streams, first_tiles):
                *hgrn_in, qb, kb, vb, k, v = _abin(h, first, st['n'], _layer(p['ab_w_in'], e), p['ab_lb'],
                                                   row('ab_g_q', e), row('ab_g_k', e), e, tile)
                state = None if st['hgrn_state'] is None else st['hgrn_state'][e]
                sb_k = None if st['sb_k'] is None else st['sb_k'][e]
                sb_v = None if st['sb_v'] is None else st['sb_v'][e]
                if st['seq'] < min(HGRN_CHUNK, SB_BLOCK) and sb_k is not None:
                    ma, s_new, mb = _short_sequence_mixers(hgrn_in, state, p['ab_g_out'][e], p['ab_lb'], e,
                                                           qb, kb, vb, sb_k, sb_v, st['batch'], st['seq'])
                else:
                    ma, s_new = _hgrn(hgrn_in, state, p['ab_g_out'][e], p['ab_lb'], e, st['batch'], st['seq'])
                    mb = _stick_breaking(qb, kb, vb, sb_k, sb_v, st['batch'], st['seq'])
                mix_a.append(ma)
                mix_b.append(mb)
                st['new_k'].append(k.reshape(st['batch'], st['seq'], N_HEADS, HEAD_DIM))
                st['new_v'].append(v.reshape(st['batch'], st['seq'], N_HEADS, HEAD_DIM))
                st['new_s'].append(s_new)
            pre = [(together(mix_a), _layer(p['ab_w_out'], e, 0, 2)), (together(mix_b), _layer(p['ab_w_out'], e, 1, 2))]
        else:
            o = l // 2
            ys = []
            for st, first in zip(streams, first_tiles):
                length = min(st['seq'], GMLP_LEN)
                assert st['n'] % GMLP_TOKEN_TILE == 0 and (first * tile) % GMLP_TOKEN_TILE == 0
                y, cv = _gmlp(h, first * tile // GMLP_TOKEN_TILE, st['n'], _layer(p['c_w_in'], o),
                              row('c_ln_g', o), row('c_ln_b', o), p['c_w_s'][o, :, :length, :length],
                              p['c_b_s'][o, :, :length], length, GMLP_TOKEN_TILE, st['want_cv'])
                ys.append(y)
                if st['want_cv']:
                    st['new_cv'].append(cv.reshape(st['batch'], st['seq'], -1))
            pre = [(together(ys), _layer(p['c_w_out'], o))]
        if l + 1 < depth:
            xa, _ = _ffn(x, pre, row('ffn2_norm', l), _layer(p['ffn2_w_gate'], l), _layer(p['ffn2_w_up'], l),
                         _layer(p['ffn2_w_down'], l), None, tile)
            x = _whole(xa, tile)
            pre = []
    l = depth - 1
    outs = []
    for i, (st, first) in enumerate(zip(streams, first_tiles)):
        xi = [(x[0][0], first, st['n'] // tile)]
        pre_i = [([m[i]], w) for m, w in pre]
        y, _ = _ffn(xi, pre_i, row('ffn2_norm', l), _layer(p['ffn2_w_gate'], l), _layer(p['ffn2_w_up'], l),
                    _layer(p['ffn2_w_down'], l), None, tile)
        outs.append(y.reshape(st['batch'], st['seq'], d))
    return outs


_MATMUL_WEIGHTS = ('ffn1_w_gate', 'ffn1_w_up', 'ffn1_w_down', 'ffn2_w_gate', 'ffn2_w_up',
                   'ffn2_w_down', 'ab_w_in', 'ab_w_out', 'c_w_in', 'c_w_out')


def kernel(x_prompt, x_sample, cache_sb_k, cache_sb_v, state_hgrn, ffn1_norm, ffn1_w_gate, ffn1_w_up, ffn1_w_down, mix_norm, ffn2_norm, ffn2_w_gate, ffn2_w_up, ffn2_w_down, ab_w_in, ab_lb, ab_g_out, ab_g_q, ab_g_k, ab_w_out, c_w_in, c_ln_g, c_ln_b, c_w_s, c_b_s, c_w_out):
    p = {'ffn1_norm': ffn1_norm, 'ffn1_w_gate': ffn1_w_gate, 'ffn1_w_up': ffn1_w_up, 'ffn1_w_down': ffn1_w_down,
         'mix_norm': mix_norm, 'ffn2_norm': ffn2_norm, 'ffn2_w_gate': ffn2_w_gate, 'ffn2_w_up': ffn2_w_up,
         'ffn2_w_down': ffn2_w_down, 'ab_w_in': ab_w_in, 'ab_lb': ab_lb, 'ab_g_out': ab_g_out, 'ab_g_q': ab_g_q,
         'ab_g_k': ab_g_k, 'ab_w_out': ab_w_out, 'c_w_in': c_w_in, 'c_ln_g': c_ln_g, 'c_ln_b': c_ln_b,
         'c_w_s': c_w_s, 'c_b_s': c_b_s, 'c_w_out': c_w_out}
    for name in _MATMUL_WEIGHTS:
        p[name] = p[name].astype(BF16)
    prompt = {'x': x_prompt, 'hgrn_state': None, 'sb_k': None, 'sb_v': None, 'want_cv': False}
    sample = {'x': x_sample, 'hgrn_state': state_hgrn, 'sb_k': cache_sb_k, 'sb_v': cache_sb_v, 'want_cv': True}
    y_prompt, y_sample = _trunk([prompt, sample], p)
    return (y_prompt, y_sample, jnp.stack(prompt['new_k']), jnp.stack(prompt['new_v']), jnp.stack(prompt['new_s']),
            jnp.stack(sample['new_k']), jnp.stack(sample['new_v']), jnp.stack(sample['new_s']),
            jnp.stack(sample['new_cv']))
```

```python
import functools

import jax
import jax.numpy as jnp
from jax import lax
from jax.experimental import pallas as pl
from jax.experimental.pallas import tpu as pltpu

F32 = jnp.float32
BF16 = jnp.bfloat16

EPS = 1e-6
FFN_SCALE = 0.5
HEAD_DIM = 128
N_HEADS = 8
CHUNK = 64
GMLP_LEN = 128
GMLP_GROUPS = 8
V7X_VMEM_LIMIT_BYTES = 56 * 1024 * 1024
TOKEN_TILE = 512
GMLP_TOKEN_TILE = 1024
FFN_ROW_GROUPS = 2
AB_ROW_GROUPS = 2
HGRN_CHUNK = 128
HGRN_HEADS_PER_STEP = 4
HGRN_UNROLL = 8
HGRN_SAFE_EXPONENT = 60.0
SB_BLOCK = 256
SB_HEADS_PER_STEP = 4
SB_EXIT = -105.0


def _params(*sem):
    return pltpu.CompilerParams(dimension_semantics=sem, vmem_limit_bytes=V7X_VMEM_LIMIT_BYTES)


def _resident(shape):
    zeros = (0,) * len(shape)
    return pl.BlockSpec(shape, lambda *_: zeros, pipeline_mode=pl.Buffered(1))


def _layer(stacked, layer, row_block=0, n_row_blocks=1):
    rows = stacked.shape[1] // n_row_blocks
    block = (None, rows) + stacked.shape[2:]
    index = (layer, row_block) + (0,) * (stacked.ndim - 2)
    return stacked, pl.BlockSpec(block, lambda *_: index, pipeline_mode=pl.Buffered(1))


def _rows(tile, width, first_tile=0):
    return pl.BlockSpec((tile, width), lambda i: (i + first_tile, 0))


def _whole(arr, tile):
    return [(arr, 0, arr.shape[0] // tile)]


def _segment_index(start, first_tile, n_tiles, i):
    return jnp.clip(i - start, 0, n_tiles - 1) + first_tile, 0


def _segment_specs(segments, tile, total_tiles):
    specs, start = [], 0
    for arr, first_tile, n_tiles in segments:
        specs.append(pl.BlockSpec((tile, arr.shape[1]),
                                  functools.partial(_segment_index, start, first_tile, n_tiles)))
        start += n_tiles
    assert start == total_tiles and len(segments) <= 2
    return specs


def _segment_rows(refs, n_first, rows):
    if len(refs) == 1:
        return refs[0][rows, :]
    return jnp.where(pl.program_id(0) < n_first, refs[0][rows, :], refs[1][rows, :])


def _sigmoid(x):
    return 1.0 / (1.0 + jnp.exp(-x))


def _rms_rows(x, g):
    ms = jnp.mean(x * x, axis=-1, keepdims=True)
    return x * lax.rsqrt(ms + EPS) * g


def _dot(a, b):
    return jnp.dot(a, b, preferred_element_type=F32)


def _dot_nt(a, b):
    return lax.dot_general(a, b, (((1,), (1,)), ((), ())), preferred_element_type=F32)


def _dot_tn(a, b):
    return lax.dot_general(a, b, (((0,), (0,)), ((), ())), preferred_element_type=F32)


def _split_bf16(x):
    hi = x.astype(BF16)
    lo = (x - hi.astype(F32)).astype(BF16)
    return hi, lo


def _head_rows(head, n):
    return pl.ds(head, n, stride=N_HEADS)


def _ffn_body(seg_counts, n_first, has_next, *refs):
    pos = seg_counts[0]
    x_refs = refs[:pos]
    pre = []
    for count in seg_counts[1:]:
        pre.append((refs[pos:pos + count], refs[pos + count]))
        pos += count + 1
    g_ref, wg_ref, wu_ref, wd_ref = refs[pos:pos + 4]
    pos += 4
    g2_ref = refs[pos] if has_next else None
    pos += int(has_next)
    xo_ref = refs[pos]
    ho_ref = refs[pos + 1] if has_next else None

    tile = xo_ref.shape[0]
    parts = [slice(k * (tile // FFN_ROW_GROUPS), (k + 1) * (tile // FFN_ROW_GROUPS)) for k in range(FFN_ROW_GROUPS)]
    xs = [_segment_rows(x_refs, n_first, r) for r in parts]
    for m_refs, w_ref in pre:
        xs = [x + _dot(_segment_rows(m_refs, n_first, r), w_ref[...]) for x, r in zip(xs, parts)]
    hs = [_rms_rows(x, g_ref[...]).astype(BF16) for x in xs]
    gates = [_dot(h, wg_ref[...]) for h in hs]
    ups = [_dot(h, wu_ref[...]) for h in hs]
    acts = [(a * _sigmoid(a) * b).astype(BF16) for a, b in zip(gates, ups)]
    downs = [_dot(act, wd_ref[...]) for act in acts]
    for x, y, r in zip(xs, downs, parts):
        x = x + FFN_SCALE * y
        xo_ref[r, :] = x
        if has_next:
            ho_ref[r, :] = _rms_rows(x, g2_ref[...]).astype(BF16)


def _ffn(x, pre, g, wg, wu, wd, g_next, tile):
    n_tiles = sum(seg[2] for seg in x)
    d = x[0][0].shape[1]
    has_next = g_next is not None
    operands = [x] + [m for m, _ in pre]
    n_first = max(segs[0][2] if len(segs) == 2 else 0 for segs in operands)
    assert all(len(segs) == 1 or segs[0][2] == n_first for segs in operands)
    in_specs = _segment_specs(x, tile, n_tiles)
    args = [seg[0] for seg in x]
    for m, (w, w_spec) in pre:
        in_specs += _segment_specs(m, tile, n_tiles) + [w_spec]
        args += [seg[0] for seg in m] + [w]
    for arr, spec in (g, wg, wu, wd):
        in_specs.append(spec)
        args.append(arr)
    out_shape = [jax.ShapeDtypeStruct((n_tiles * tile, d), F32)]
    out_specs = [_rows(tile, d)]
    if has_next:
        in_specs.append(g_next[1])
        args.append(g_next[0])
        out_shape.append(jax.ShapeDtypeStruct((n_tiles * tile, d), BF16))
        out_specs.append(_rows(tile, d))
    outs = pl.pallas_call(
        functools.partial(_ffn_body, tuple(len(segs) for segs in operands), n_first, has_next),
        out_shape=out_shape, grid=(n_tiles,), in_specs=in_specs, out_specs=out_specs,
        compiler_params=_params("parallel"), name="ffn")(*args)
    return (outs[0], outs[1]) if has_next else (outs[0], None)


def _abin_body(layer, h_ref, w_ref, lbp_ref, gq_ref, gk_ref,
               qa_ref, lf_ref, va_ref, ga_ref, qb_ref, kb_ref, vb_ref, k_ref, v_ref):
    tile, d = h_ref.shape
    group = tile // AB_ROW_GROUPS

    lbp = lbp_ref[...]
    e = jnp.exp(lbp - jnp.max(lbp, axis=0, keepdims=True))
    lb = jnp.sum(e[:layer + 1], axis=0, keepdims=True) / jnp.sum(e, axis=0, keepdims=True)

    for k in range(AB_ROW_GROUPS):
        r = slice(k * group, (k + 1) * group)
        h = h_ref[r, :]

        def seg(i, h=h):
            return _dot(h, w_ref[:, i * d:(i + 1) * d])

        qa = seg(0)
        qa_ref[r, :] = (qa * _sigmoid(qa)).astype(BF16)
        f = lb + (1.0 - lb) * _sigmoid(seg(1))
        lf_ref[r, :] = jnp.log(f)
        va_ref[r, :] = seg(2).astype(BF16)
        ga_ref[r, :] = _sigmoid(seg(3)).astype(BF16)

        qb = seg(4)
        kb = seg(5)
        vb = seg(6)
        vb_ref[r, :] = vb.astype(BF16)
        for hh in range(N_HEADS):
            sl = slice(hh * HEAD_DIM, (hh + 1) * HEAD_DIM)
            qb_ref[r, sl] = (_rms_rows(qb[:, sl], gq_ref[...]) * HEAD_DIM ** -0.5).astype(BF16)
            kn = _rms_rows(kb[:, sl], gk_ref[...])
            kb_ref[r, sl] = kn.astype(BF16)
            head_rows = pl.ds(k * group * N_HEADS + hh, group, stride=N_HEADS)
            k_ref[head_rows, :] = kn
            v_ref[head_rows, :] = vb[:, sl]


def _abin(h, first_tile, n, w_in, lb_param, g_q, g_k, layer, tile):
    d = h.shape[1]
    assert d == N_HEADS * HEAD_DIM
    feat = [BF16, F32, BF16, BF16, BF16, BF16, BF16]
    out_shape = [jax.ShapeDtypeStruct((n, d), t) for t in feat]
    out_specs = [_rows(tile, d) for _ in feat]
    for _ in range(2):
        out_shape.append(jax.ShapeDtypeStruct((n * N_HEADS, HEAD_DIM), F32))
        out_specs.append(_rows(tile * N_HEADS, HEAD_DIM))
    return pl.pallas_call(
        functools.partial(_abin_body, layer),
        out_shape=out_shape, grid=(n // tile,),
        in_specs=[_rows(tile, d, first_tile), w_in[1], _resident(lb_param.shape), g_q[1], g_k[1]],
        out_specs=out_specs,
        compiler_params=_params("parallel"), name="ab_in_proj")(
            h, w_in[0], lb_param, g_q[0], g_k[0])


def _hgrn_scores_any_decay(q, k, lf2, chunk):
    row = lax.broadcasted_iota(jnp.int32, (chunk, chunk), 0)
    col = lax.broadcasted_iota(jnp.int32, (chunk, chunk), 1)
    t1 = lax.broadcasted_iota(jnp.int32, (chunk, 1), 0)
    scores = jnp.where(row == col, jnp.sum(q * k, axis=-1, keepdims=True), 0.0)
    half = chunk // 2
    while half >= 1:
        blk = 2 * half
        same = (row // blk) == (col // blk)
        mid = (row // blk) * blk + half
        right = row >= mid
        seg = same & ((right & (col >= mid) & (col <= row)) | ((col > row) & (col < mid)))
        d2 = _dot(jnp.where(seg, 1.0, 0.0).astype(BF16), lf2)
        decay = jnp.exp(d2[:, :HEAD_DIM] + d2[:, HEAD_DIM:])
        right1 = (t1 % blk) >= half
        qe = jnp.where(right1, q * decay, 0.0).astype(BF16)
        ke = jnp.where(right1, 0.0, k * decay).astype(BF16)
        scores = scores + jnp.where(same, _dot_nt(qe, ke), 0.0)
        half //= 2
    return scores


def _hgrn_body(chunk, n_chunks, heads, has_s0, factored, *refs):
    q_ref, lf_ref, v_ref, gt_ref = refs[:4]
    pos = 4
    s0_ref = refs[pos] if has_s0 else None
    pos += int(has_s0)
    go_ref, o_ref, so_ref = refs[pos:pos + 3]

    row = lax.broadcasted_iota(jnp.int32, (chunk, chunk), 0)
    col = lax.broadcasted_iota(jnp.int32, (chunk, chunk), 1)
    causal = col <= row
    lower = jnp.where(causal, 1.0, 0.0).astype(BF16)
    mid = chunk // 2
    head_cols = [slice(j * HEAD_DIM, (j + 1) * HEAD_DIM) for j in range(heads)]

    def step(i, states):
        rows = pl.ds(pl.multiple_of(i * chunk, chunk), chunk)
        lfs = [lf_ref[rows, cl] for cl in head_cols]
        lf2s = [jnp.concatenate(_split_bf16(lf), axis=1) for lf in lfs]
        c2s = [_dot(lower, lf2) for lf2 in lf2s]
        qs = [q_ref[rows, cl].astype(F32) for cl in head_cols]
        ks = [1.0 - jnp.exp(lf) for lf in lfs]
        cs = [c2[:, :HEAD_DIM] + c2[:, HEAD_DIM:] for c2 in c2s]
        c_lasts = [c[chunk - 1:chunk, :] for c in cs]
        vs = [v_ref[rows, cl] for cl in head_cols]
        if factored:
            c_mids = [c[mid - 1:mid, :] for c in cs]
            rel = [c - cm for c, cm in zip(cs, c_mids)]
            qes = [(q * jnp.exp(a)).astype(BF16) for q, a in zip(qs, rel)]
            kes = [(k * jnp.exp(-a)).astype(BF16) for k, a in zip(ks, rel)]
            rhs = [jnp.concatenate([ke, (st * jnp.exp(cm)).astype(BF16)], axis=0)
                   for ke, st, cm in zip(kes, states, c_mids)]
            both = [_dot_nt(qe, r) for qe, r in zip(qes, rhs)]
            grown = [_dot_tn(v, ke) * jnp.exp(cl - cm) for v, ke, cl, cm in zip(vs, kes, c_lasts, c_mids)]
            scores = [jnp.where(causal, bt[:, :chunk], 0.0) for bt in both]
            inter = [bt[:, chunk:] for bt in both]
        else:
            inter = [_dot_nt((q * jnp.exp(c)).astype(BF16), st.astype(BF16)) for q, c, st in zip(qs, cs, states)]
            scores = [_hgrn_scores_any_decay(q, k, lf2, chunk) for q, k, lf2 in zip(qs, ks, lf2s)]
            kds = [(k * jnp.exp(cl - c)).astype(BF16) for k, cl, c in zip(ks, c_lasts, cs)]
            grown = [_dot_tn(v, kd) for v, kd in zip(vs, kds)]
        intra = [_dot(s.astype(BF16), v) for s, v in zip(scores, vs)]
        for j, cl in enumerate(head_cols):
            o = _rms_rows(inter[j] + intra[j], go_ref[j]) * gt_ref[rows, cl].astype(F32)
            o_ref[rows, cl] = o.astype(o_ref.dtype)
        return tuple(st * jnp.exp(cl) + g for st, cl, g in zip(states, c_lasts, grown))

    if has_s0:
        states = tuple(s0_ref[j].T for j in range(heads))
    else:
        states = tuple(jnp.zeros((HEAD_DIM, HEAD_DIM), F32) for _ in range(heads))
    if n_chunks > 1:
        states = lax.fori_loop(0, n_chunks, step, states, unroll=min(n_chunks, HGRN_UNROLL))
    else:
        states = step(0, states)
    for j in range(heads):
        so_ref[j] = states[j].T


def _hgrn_plan(tokens, s0, g_out, batch, seq, chunk, heads, factored):
    n, d = tokens[0].shape
    has_s0 = s0 is not None
    width = heads * HEAD_DIM
    tok = pl.BlockSpec((seq, width), lambda b, h: (b, h))
    state = pl.BlockSpec((None, heads, HEAD_DIM, HEAD_DIM), lambda b, h: (b, h, 0, 0))
    in_specs = [tok] * len(tokens)
    args = list(tokens)
    if has_s0:
        in_specs.append(state)
        args.append(s0)
    in_specs.append(pl.BlockSpec((heads, 1, HEAD_DIM), lambda b, h: (h, 0, 0)))
    args.append(g_out.reshape(N_HEADS, 1, HEAD_DIM))
    return dict(body=functools.partial(_hgrn_body, chunk, seq // chunk, heads, has_s0, factored),
                out_shape=[jax.ShapeDtypeStruct((n, d), BF16),
                           jax.ShapeDtypeStruct((batch, N_HEADS, HEAD_DIM, HEAD_DIM), F32)],
                grid=(batch, N_HEADS // heads), in_specs=in_specs, out_specs=[tok, state], scratch=[], args=args)


def _run(plan, name):
    return pl.pallas_call(plan['body'], out_shape=plan['out_shape'], grid=plan['grid'], in_specs=plan['in_specs'],
                          out_specs=plan['out_specs'], scratch_shapes=plan['scratch'],
                          compiler_params=_params("parallel", "parallel"), name=name)(*plan['args'])


def _hgrn_shape(seq):
    return min(seq, HGRN_CHUNK), (N_HEADS if seq < HGRN_CHUNK else HGRN_HEADS_PER_STEP)


def _hgrn_can_factor(lb_param, layer, chunk):
    lb = jnp.cumsum(jax.nn.softmax(lb_param.astype(F32), axis=0), axis=0)[layer]
    return (chunk // 2) * jnp.max(-jnp.log(lb)) <= HGRN_SAFE_EXPONENT


def _hgrn(tokens, s0, g_out, lb_param, layer, batch, seq):
    chunk, heads = _hgrn_shape(seq)
    call = lambda factored: _run(_hgrn_plan(tokens, s0, g_out, batch, seq, chunk, heads, factored), "hgrn2")
    return lax.cond(_hgrn_can_factor(lb_param, layer, chunk), lambda: call(True), lambda: call(False))


def _sb_logits_many(items):
    zs = [_dot_nt(q, kb) for q, kb, _, _ in items]
    sps = [jnp.maximum(z, 0.0) + jnp.log(1.0 + jnp.exp(-jnp.abs(z))) for z in zs]
    costs = [sp if it[3] is None else jnp.where(it[3], sp, 0.0) for sp, it in zip(sps, items)]
    suffix = [_dot(c.astype(BF16), it[2]) for c, it in zip(costs, items)]
    return [(z - sp - sf, jnp.sum(c, axis=-1, keepdims=True)) for z, sp, sf, c in zip(zs, sps, suffix, costs)]


def _sb_accumulate(logits, cost, vb, mask, acc, spent):
    w = jnp.exp(logits - spent)
    if mask is not None:
        w = jnp.where(mask, w, 0.0)
    return acc + _dot(w.astype(BF16), vb), spent + cost


def _strict_upper(n):
    j = lax.broadcasted_iota(jnp.int32, (n, n), 0)
    k = lax.broadcasted_iota(jnp.int32, (n, n), 1)
    return jnp.where(j > k, 1.0, 0.0).astype(BF16)


def _sb_body(blk, n_q, heads, past_blk, n_past, *refs):
    q_ref, k_ref, v_ref = refs[:3]
    if n_past:
        kp_new_ref, vp_new_ref, kp_hbm, vp_hbm, o_ref, kbuf, vbuf, sem = refs[3:]
    else:
        o_ref = refs[3]

    t = lax.broadcasted_iota(jnp.int32, (blk, blk), 0)
    s = lax.broadcasted_iota(jnp.int32, (blk, blk), 1)
    diag_mask = s < t
    upper = _strict_upper(blk)
    upper_past = _strict_upper(past_blk) if n_past else None
    head_cols = [slice(j * HEAD_DIM, (j + 1) * HEAD_DIM) for j in range(heads)]

    def still_live(spent):
        return functools.reduce(jnp.minimum, [jnp.min(r) for r in spent]) < -SB_EXIT

    for qi in range(n_q):
        rows = pl.ds(qi * blk, blk)
        qs = [q_ref[rows, c] for c in head_cols]
        n_older = qi + n_past
        items = [(qs[j], k_ref[rows, c], upper, diag_mask) for j, c in enumerate(head_cols)]
        values = [(v_ref[rows, c], diag_mask) for c in head_cols]
        if qi > 0:
            prev = pl.ds((qi - 1) * blk, blk)
            items += [(qs[j], k_ref[prev, c], upper, None) for j, c in enumerate(head_cols)]
            values += [(v_ref[prev, c], None) for c in head_cols]
        elif n_past:
            items += [(qs[j], kp_new_ref[_head_rows(j, past_blk), :].astype(BF16), upper_past, None)
                      for j in range(heads)]
            values += [(vp_new_ref[_head_rows(j, past_blk), :].astype(BF16), None) for j in range(heads)]
        logits = _sb_logits_many(items)
        accs = [jnp.zeros((blk, HEAD_DIM), F32) for _ in range(heads)]
        runs = [jnp.zeros((blk, 1), F32) for _ in range(heads)]
        for n, ((lg, stay), (vb, mask)) in enumerate(zip(logits, values)):
            j = n % heads
            accs[j], runs[j] = _sb_accumulate(lg, stay, vb, mask, accs[j], runs[j])

        if n_older > 1:
            def cond(carry):
                return (carry[0] < n_older) & carry[1]

            def body(carry, qi=qi, qs=qs):
                idx = carry[0]
                accs, runs = list(carry[2]), list(carry[3])
                if n_past:
                    src = pl.ds(pl.multiple_of((n_past - 1 - idx) * past_blk * N_HEADS, past_blk * N_HEADS),
                                past_blk * N_HEADS)
                    b = pl.program_id(0)
                    copies = [pltpu.make_async_copy(kp_hbm.at[b, src, :], kbuf, sem.at[0]),
                              pltpu.make_async_copy(vp_hbm.at[b, src, :], vbuf, sem.at[1])]
                    for cp in copies:
                        cp.start()
                    for cp in copies:
                        cp.wait()
                if n_past:
                    items = [(qs[j], kbuf[_head_rows(j, past_blk), :].astype(BF16), upper_past, None)
                             for j in range(heads)]
                    vbs = [vbuf[_head_rows(j, past_blk), :].astype(BF16) for j in range(heads)]
                else:
                    kr = pl.ds(pl.multiple_of((qi - 1 - idx) * blk, blk), blk)
                    items = [(qs[j], k_ref[kr, c], upper, None) for j, c in enumerate(head_cols)]
                    vbs = [v_ref[kr, c] for c in head_cols]
                for j, (lg, stay) in enumerate(_sb_logits_many(items)):
                    accs[j], runs[j] = _sb_accumulate(lg, stay, vbs[j], None, accs[j], runs[j])
                return idx + 1, still_live(runs), tuple(accs), tuple(runs)

            _, _, accs, _ = lax.while_loop(cond, body, (jnp.int32(1), still_live(runs), tuple(accs), tuple(runs)))
        for j, c in enumerate(head_cols):
            o_ref[rows, c] = accs[j].astype(o_ref.dtype)


def _sb_heads(seq):
    return N_HEADS if seq < SB_BLOCK else SB_HEADS_PER_STEP


def _sb_plan(qn, kb, vb, k_past, v_past, batch, seq):
    n, d = qn.shape
    blk = min(seq, SB_BLOCK)
    n_q = seq // blk
    heads = _sb_heads(seq)
    tok = pl.BlockSpec((seq, heads * HEAD_DIM), lambda b, h: (b, h))
    in_specs = [tok, tok, tok]
    args = [qn, kb, vb]
    scratch = []
    past_blk = n_past = 0
    if k_past is not None:
        assert n_q == 1 and heads == N_HEADS, "cached keys are only supported for a single query block"
        past_len = k_past.shape[1]
        past_blk = min(past_len, SB_BLOCK)
        n_past = past_len // past_blk
        slab = past_blk * N_HEADS
        newest = pl.BlockSpec((None, slab, HEAD_DIM), lambda b, h: (b, n_past - 1, 0))
        hbm = pl.BlockSpec(memory_space=pl.ANY)
        in_specs += [newest, newest, hbm, hbm]
        kp = k_past.reshape(batch, past_len * N_HEADS, HEAD_DIM)
        vp = v_past.reshape(batch, past_len * N_HEADS, HEAD_DIM)
        args += [kp, vp, kp, vp]
        scratch = [pltpu.VMEM((slab, HEAD_DIM), F32), pltpu.VMEM((slab, HEAD_DIM), F32),
                   pltpu.SemaphoreType.DMA((2,))]
    return dict(body=functools.partial(_sb_body, blk, n_q, heads, past_blk, n_past),
                out_shape=[jax.ShapeDtypeStruct((n, d), BF16)], grid=(batch, N_HEADS // heads),
                in_specs=in_specs, out_specs=[tok], scratch=scratch, args=args)


def _stick_breaking(qn, kb, vb, k_past, v_past, batch, seq):
    return _run(_sb_plan(qn, kb, vb, k_past, v_past, batch, seq), "stick_breaking")[0]


def _both_body(first, second, *refs):
    n_in = len(first['in_specs']) + len(second['in_specs'])
    n_out = len(first['out_specs']) + len(second['out_specs'])
    ins, outs, scratch = refs[:n_in], refs[n_in:n_in + n_out], refs[n_in + n_out:]
    first['body'](*ins[:len(first['in_specs'])], *outs[:len(first['out_specs'])])
    second['body'](*ins[len(first['in_specs']):], *outs[len(first['out_specs']):], *scratch)


def _short_sequence_mixers(tokens, s0, g_out, lb_param, layer, qn, kb, vb, k_past, v_past, batch, seq):
    chunk, heads = _hgrn_shape(seq)

    def call(factored):
        first = _hgrn_plan(tokens, s0, g_out, batch, seq, chunk, heads, factored)
        second = _sb_plan(qn, kb, vb, k_past, v_past, batch, seq)
        assert first['grid'] == second['grid'] and not first['scratch']
        both = dict(body=functools.partial(_both_body, first, second),
                    out_shape=first['out_shape'] + second['out_shape'], grid=first['grid'],
                    in_specs=first['in_specs'] + second['in_specs'],
                    out_specs=first['out_specs'] + second['out_specs'],
                    scratch=second['scratch'], args=first['args'] + second['args'])
        return tuple(_run(both, "short_sequence_mixers"))

    return lax.cond(_hgrn_can_factor(lb_param, layer, chunk), lambda: call(True), lambda: call(False))


def _twice_gelu_tanh(x):
    t = jnp.tanh(x * (0.7978845608028654 + (0.7978845608028654 * 0.044715) * (x * x)))
    return x + x * t


def _gmlp_body(length, n_sub, emit_v, h_ref, w_ref, g_ref, b_ref, ws_ref, bs_ref, y_ref, *v_out):
    h = h_ref[...]
    d_c = w_ref.shape[1] // 2
    dg = d_c // GMLP_GROUPS
    i = lax.broadcasted_iota(jnp.int32, (length, length), 0)
    j = lax.broadcasted_iota(jnp.int32, (length, length), 1)
    mask = (j // CHUNK) <= (i // CHUNK)
    half_bias = 0.5 * bs_ref[...]

    vs = [_twice_gelu_tanh(_dot(h, w_ref[:, d_c + g * dg:d_c + (g + 1) * dg])) for g in range(GMLP_GROUPS)]
    total = functools.reduce(jnp.add, [jnp.sum(v, axis=-1, keepdims=True) for v in vs])
    total_sq = functools.reduce(jnp.add, [jnp.sum(v * v, axis=-1, keepdims=True) for v in vs])
    mean = total / d_c
    inv = lax.rsqrt(total_sq / d_c - mean * mean + 4.0 * EPS)
    for g in range(GMLP_GROUPS):
        cols = slice(g * dg, (g + 1) * dg)
        vn = (vs[g] - mean) * inv * g_ref[:, cols] + b_ref[:, cols]
        if emit_v:
            v_out[0][:, cols] = vn
        vn = vn.astype(BF16)
        u = _twice_gelu_tanh(_dot(h, w_ref[:, cols]))
        w = jnp.where(mask, 0.5 * ws_ref[g], 0.0).astype(BF16)
        for c in range(n_sub):
            rows = slice(c * length, (c + 1) * length)
            mixed = _dot(w, vn[rows]) + half_bias[:, g:g + 1]
            y_ref[rows, cols] = (u[rows] * mixed).astype(BF16)


def _gmlp(h, first_tile, n, w_in, ln_g, ln_b, w_s, b_s, length, tile, emit_v):
    d = h.shape[1]
    d_c = w_in[0].shape[-1] // 2
    bs = b_s.T
    out_shape = [jax.ShapeDtypeStruct((n, d_c), BF16)]
    out_specs = [_rows(tile, d_c)]
    if emit_v:
        out_shape.append(jax.ShapeDtypeStruct((n, d_c), F32))
        out_specs.append(_rows(tile, d_c))
    outs = pl.pallas_call(
        functools.partial(_gmlp_body, length, tile // length, emit_v),
        out_shape=out_shape, grid=(n // tile,),
        in_specs=[_rows(tile, d, first_tile), w_in[1], ln_g[1], ln_b[1], _resident(w_s.shape), _resident(bs.shape)],
        out_specs=out_specs,
        compiler_params=_params("parallel"), name="gmlp")(h, w_in[0], ln_g[0], ln_b[0], w_s, bs)
    return (outs[0], outs[1]) if emit_v else (outs[0], None)


def _trunk(streams, p):
    d = streams[0]['x'].shape[-1]
    tile = TOKEN_TILE
    depth = p['ffn1_norm'].shape[0]
    for st in streams:
        st['batch'], st['seq'], _ = st['x'].shape
        st['n'] = st['batch'] * st['seq']
        assert st['n'] % tile == 0
        st['new_k'], st['new_v'], st['new_s'], st['new_cv'] = [], [], [], []
    first_tiles = [sum(st['n'] for st in streams[:i]) // tile for i in range(len(streams))]

    def row(name, l):
        v = p[name]
        return _layer(v.reshape(v.shape[0], 1, -1), l)

    def together(arrays):
        return [(a, 0, a.shape[0] // tile) for a in arrays]

    x = together([st['x'].reshape(st['n'], d) for st in streams])
    pre = []
    for l in range(depth):
        xa, h = _ffn(x, pre, row('ffn1_norm', l), _layer(p['ffn1_w_gate'], l), _layer(p['ffn1_w_up'], l),
                     _layer(p['ffn1_w_down'], l), row('mix_norm', l), tile)
        x = _whole(xa, tile)
        if l % 2 == 0:
            e = l // 2
            mix_a, mix_b = [], []
            for st, first in zip(streams, first_tiles):
                *hgrn_in, qb, kb, vb, k, v = _abin(h, first, st['n'], _layer(p['ab_w_in'], e), p['ab_lb'],
                                                   row('ab_g_q', e), row('ab_g_k', e), e, tile)
                state = None if st['hgrn_state'] is None else st['hgrn_state'][e]
                sb_k = None if st['sb_k'] is None else st['sb_k'][e]
                sb_v = None if st['sb_v'] is None else st['sb_v'][e]
                if _hgrn_shape(st['seq'])[1] == _sb_heads(st['seq']):
                    ma, s_new, mb = _short_sequence_mixers(hgrn_in, state, p['ab_g_out'][e], p['ab_lb'], e,
                                                           qb, kb, vb, sb_k, sb_v, st['batch'], st['seq'])
                else:
                    ma, s_new = _hgrn(hgrn_in, state, p['ab_g_out'][e], p['ab_lb'], e, st['batch'], st['seq'])
                    mb = _stick_breaking(qb, kb, vb, sb_k, sb_v, st['batch'], st['seq'])
                mix_a.append(ma)
                mix_b.append(mb)
                st['new_k'].append(k.reshape(st['batch'], st['seq'], N_HEADS, HEAD_DIM))
                st['new_v'].append(v.reshape(st['batch'], st['seq'], N_HEADS, HEAD_DIM))
                st['new_s'].append(s_new)
            pre = [(together(mix_a), _layer(p['ab_w_out'], e, 0, 2)), (together(mix_b), _layer(p['ab_w_out'], e, 1, 2))]
        else:
            o = l // 2
            ys = []
            for st, first in zip(streams, first_tiles):
                length = min(st['seq'], GMLP_LEN)
                assert st['n'] % GMLP_TOKEN_TILE == 0 and (first * tile) % GMLP_TOKEN_TILE == 0
                y, cv = _gmlp(h, first * tile // GMLP_TOKEN_TILE, st['n'], _layer(p['c_w_in'], o),
                              row('c_ln_g', o), row('c_ln_b', o), p['c_w_s'][o, :, :length, :length],
                              p['c_b_s'][o, :, :length], length, GMLP_TOKEN_TILE, st['want_cv'])
                ys.append(y)
                if st['want_cv']:
                    st['new_cv'].append(cv.reshape(st['batch'], st['seq'], -1))
            pre = [(together(ys), _layer(p['c_w_out'], o))]
        if l + 1 < depth:
            xa, _ = _ffn(x, pre, row('ffn2_norm', l), _layer(p['ffn2_w_gate'], l), _layer(p['ffn2_w_up'], l),
                         _layer(p['ffn2_w_down'], l), None, tile)
            x = _whole(xa, tile)
            pre = []
    l = depth - 1
    outs = []
    for i, (st, first) in enumerate(zip(streams, first_tiles)):
        xi = [(x[0][0], first, st['n'] // tile)]
        pre_i = [([m[i]], w) for m, w in pre]
        y, _ = _ffn(xi, pre_i, row('ffn2_norm', l), _layer(p['ffn2_w_gate'], l), _layer(p['ffn2_w_up'], l),
                    _layer(p['ffn2_w_down'], l), None, tile)
        outs.append(y.reshape(st['batch'], st['seq'], d))
    return outs


_MATMUL_WEIGHTS = ('ffn1_w_gate', 'ffn1_w_up', 'ffn1_w_down', 'ffn2_w_gate', 'ffn2_w_up',
                   'ffn2_w_down', 'ab_w_in', 'ab_w_out', 'c_w_in', 'c_w_out')


def kernel(x_prompt, x_sample, cache_sb_k, cache_sb_v, state_hgrn, ffn1_norm, ffn1_w_gate, ffn1_w_up, ffn1_w_down, mix_norm, ffn2_norm, ffn2_w_gate, ffn2_w_up, ffn2_w_down, ab_w_in, ab_lb, ab_g_out, ab_g_q, ab_g_k, ab_w_out, c_w_in, c_ln_g, c_ln_b, c_w_s, c_b_s, c_w_out):
    p = {'ffn1_norm': ffn1_norm, 'ffn1_w_gate': ffn1_w_gate, 'ffn1_w_up': ffn1_w_up, 'ffn1_w_down': ffn1_w_down,
         'mix_norm': mix_norm, 'ffn2_norm': ffn2_norm, 'ffn2_w_gate': ffn2_w_gate, 'ffn2_w_up': ffn2_w_up,
         'ffn2_w_down': ffn2_w_down, 'ab_w_in': ab_w_in, 'ab_lb': ab_lb, 'ab_g_out': ab_g_out, 'ab_g_q': ab_g_q,
         'ab_g_k': ab_g_k, 'ab_w_out': ab_w_out, 'c_w_in': c_w_in, 'c_ln_g': c_ln_g, 'c_ln_b': c_ln_b,
         'c_w_s': c_w_s, 'c_b_s': c_b_s, 'c_w_out': c_w_out}
    for name in _MATMUL_WEIGHTS:
        p[name] = p[name].astype(BF16)
    prompt = {'x': x_prompt, 'hgrn_state': None, 'sb_k': None, 'sb_v': None, 'want_cv': False}
    sample = {'x': x_sample, 'hgrn_state': state_hgrn, 'sb_k': cache_sb_k, 'sb_v': cache_sb_v, 'want_cv': True}
    y_prompt, y_sample = _trunk([prompt, sample], p)
    return (y_prompt, y_sample, jnp.stack(prompt['new_k']), jnp.stack(prompt['new_v']), jnp.stack(prompt['new_s']),
            jnp.stack(sample['new_k']), jnp.stack(sample['new_v']), jnp.stack(sample['new_s']),
            jnp.stack(sample['new_cv']))
```
